```python
import jax, jax.numpy as jnp
from jax import lax
import numpy as np

D_MODEL = 2048
BATCH = 1
SEQ = 8192
DEPTH = 2

CONV_DIM = D_MODEL // 2
CONV_WIDTH = 31
NSA_HEADS = 16
NSA_HEAD_DIM = 64
NSA_KV_HEADS = 4
NSA_GROUP = NSA_HEADS // NSA_KV_HEADS
N_BRANCH = 3
CMP_LEN = 32
CMP_STRIDE = 16
CMP_HIDDEN = 2 * NSA_HEAD_DIM
SEL_LEN = 64
SEL_TOPK = 16
WINDOW = 512
Q_BLOCK = 128
FORCE_BONUS = 1e3
NEG_INF = -1e30
POOL_WINDOWS = (2, 4, 8, 16)
POOL_GROUPS = len(POOL_WINDOWS)
POOL_DIM = D_MODEL // POOL_GROUPS
D_FF = 4 * D_MODEL
EPS = 1e-6

Q_COLS = NSA_HEADS * NSA_HEAD_DIM
KV_COLS = NSA_KV_HEADS * NSA_HEAD_DIM
GATE_COLS = NSA_HEADS * N_BRANCH
IN_COLS = 2 * CONV_DIM + Q_COLS + 2 * N_BRANCH * KV_COLS + GATE_COLS
N_EVEN = (DEPTH + 1) // 2
N_ODD = DEPTH // 2

kernel_name = "hybrid_conv_nsa_pool_block"


def rmsnorm(x, g):
    xf = x.astype(jnp.float32)
    r = lax.rsqrt(jnp.mean(xf * xf, axis=-1, keepdims=True) + EPS)
    return (xf * r * g.astype(jnp.float32)).astype(x.dtype)


def layernorm(x, g, b):
    xf = x.astype(jnp.float32)
    mu = jnp.mean(xf, axis=-1, keepdims=True)
    xc = xf - mu
    var = jnp.mean(xc * xc, axis=-1, keepdims=True)
    y = xc * lax.rsqrt(var + EPS) * g.astype(jnp.float32) + b.astype(jnp.float32)
    return y.astype(x.dtype)


def masked_softmax(s, mask):
    s = jnp.where(mask, s.astype(jnp.float32), NEG_INF)
    p = jax.nn.softmax(s, axis=-1)
    return jnp.where(mask, p, 0.0)


def causal_depthwise_conv(u, w, b):
    C = u.shape[-1]
    lhs = jnp.pad(u, ((0, 0), (CONV_WIDTH - 1, 0), (0, 0)))
    out = lax.conv_general_dilated(
        lhs, w[:, None, :].astype(u.dtype), window_strides=(1,), padding='VALID',
        dimension_numbers=('NWC', 'WIO', 'NWC'), feature_group_count=C)
    return out + b


def compress_blocks(k, pos, w1, w2):
    B, S, G, dh = k.shape
    ch = k.reshape(B, S // CMP_STRIDE, CMP_STRIDE, G, dh)
    blk = jnp.concatenate([ch[:, :-1], ch[:, 1:]], axis=2)
    blk = blk + pos[None, None, :, None, :]
    nc = blk.shape[1]
    blk = blk.transpose(0, 1, 3, 2, 4).reshape(B, nc, G, CMP_LEN * dh)
    h = jax.nn.silu(blk @ w1)
    return h @ w2


def nsa(q, k_c, v_c, k_s, v_s, k_w, v_w, gates, pos_k, pos_v, kw1, kw2, vw1, vw2):
    B, S, H, dh = q.shape
    G, R = NSA_KV_HEADS, NSA_GROUP
    scale = dh ** -0.5
    q = q.reshape(B, S, G, R, dh)
    gates = gates.reshape(B, S, G, R, N_BRANCH)

    kc = compress_blocks(k_c, pos_k, kw1, kw2)
    vc = compress_blocks(v_c, pos_v, vw1, vw2)
    nc = kc.shape[1]
    cmp_start = jnp.arange(nc) * CMP_STRIDE
    cmp_end = cmp_start + CMP_LEN - 1

    n_sel = S // SEL_LEN
    topk = min(SEL_TOPK, n_sel)
    sel_start = jnp.arange(n_sel) * SEL_LEN
    overlap = ((cmp_start[:, None] <= sel_start[None, :] + SEL_LEN - 1)
               & (cmp_end[:, None] >= sel_start[None, :])).astype(jnp.float32)
    ks_blk = k_s.reshape(B, n_sel, SEL_LEN, G, dh).transpose(0, 3, 1, 2, 4)
    vs_blk = v_s.reshape(B, n_sel, SEL_LEN, G, dh).transpose(0, 3, 1, 2, 4)
    gather = jax.vmap(jax.vmap(lambda blk, i: blk[i]))

    kw_pad = jnp.pad(k_w, ((0, 0), (WINDOW, 0), (0, 0), (0, 0)))
    vw_pad = jnp.pad(v_w, ((0, 0), (WINDOW, 0), (0, 0), (0, 0)))
    sel_j = jnp.arange(n_sel)
    sel_off = jnp.arange(SEL_LEN)
    win_off = jnp.arange(WINDOW + Q_BLOCK)

    def block_fn(bi):
        s0 = bi * Q_BLOCK
        qb = lax.dynamic_slice_in_dim(q, s0, Q_BLOCK, axis=1)
        gb = lax.dynamic_slice_in_dim(gates, s0, Q_BLOCK, axis=1)
        t = s0 + jnp.arange(Q_BLOCK)

        sc = jnp.einsum('bqgrd,bngd->bgrqn', qb, kc) * scale
        pc = masked_softmax(sc, cmp_end[None, :] <= t[:, None])
        oc = jnp.einsum('bgrqn,bngd->bqgrd', pc.astype(vc.dtype), vc)

        imp = jnp.einsum('bgrqn,nj->bgqj', pc, overlap)
        cur = t // SEL_LEN
        forced = (sel_j[None, :] == 0) | (sel_j[None, :] == cur[:, None]) | (sel_j[None, :] == cur[:, None] - 1)
        future = sel_j[None, :] > cur[:, None]
        imp = jnp.where(future, NEG_INF, imp + jnp.where(forced, FORCE_BONUS, 0.0))
        _, idx = lax.top_k(imp, topk)
        ks_sel = gather(ks_blk, idx)
        vs_sel = gather(vs_blk, idx)
        ss = jnp.einsum('bqgrd,bgqkld->bgrqkl', qb, ks_sel) * scale
        kpos = idx[..., None] * SEL_LEN + sel_off
        ms = (kpos <= t[:, None, None])[:, :, None]
        ps = masked_softmax(ss.reshape(B, G, R, Q_BLOCK, topk * SEL_LEN),
                            ms.reshape(B, G, 1, Q_BLOCK, topk * SEL_LEN))
        ps = ps.reshape(B, G, R, Q_BLOCK, topk, SEL_LEN).astype(vs_sel.dtype)
        os_ = jnp.einsum('bgrqkl,bgqkld->bqgrd', ps, vs_sel)

        kwb = lax.dynamic_slice_in_dim(kw_pad, s0, WINDOW + Q_BLOCK, axis=1)
        vwb = lax.dynamic_slice_in_dim(vw_pad, s0, WINDOW + Q_BLOCK, axis=1)
        wpos = s0 - WINDOW + win_off
        mw = ((wpos[None, :] <= t[:, None]) & (wpos[None, :] > t[:, None] - WINDOW)
              & (wpos[None, :] >= 0))
        sw = jnp.einsum('bqgrd,bkgd->bgrqk', qb, kwb) * scale
        pw = masked_softmax(sw, mw).astype(vwb.dtype)
        ow = jnp.einsum('bgrqk,bkgd->bqgrd', pw, vwb)

        return gb[..., 0:1] * oc + gb[..., 1:2] * os_ + gb[..., 2:3] * ow

    out = lax.map(block_fn, jnp.arange(S // Q_BLOCK))
    out = out.transpose(1, 0, 2, 3, 4, 5).reshape(B, S, H * dh)
    return out


def even_mixer(x, g, w_in, w_out, conv_w, conv_b, ln_g, ln_b,
               pos_k, pos_v, kw1, kw2, vw1, vw2):
    B, S, _ = x.shape
    xn = rmsnorm(x, g)
    proj = xn @ w_in
    sizes = [CONV_DIM, CONV_DIM, Q_COLS] + [KV_COLS] * (2 * N_BRANCH) + [GATE_COLS]
    cuts = list(np.cumsum(sizes)[:-1])
    a, gt, q, k_c, v_c, k_s, v_s, k_w, v_w, gl = jnp.split(proj, cuts, axis=-1)
    c = a * jax.nn.sigmoid(gt)
    c = causal_depthwise_conv(c, conv_w, conv_b)
    c = jax.nn.silu(layernorm(c, ln_g, ln_b))
    kv = lambda t_: t_.reshape(B, S, NSA_KV_HEADS, NSA_HEAD_DIM)
    o = nsa(q.reshape(B, S, NSA_HEADS, NSA_HEAD_DIM), kv(k_c), kv(v_c), kv(k_s), kv(v_s),
            kv(k_w), kv(v_w), jax.nn.sigmoid(gl.reshape(B, S, NSA_HEADS, N_BRANCH)),
            pos_k, pos_v, kw1, kw2, vw1, vw2)
    return jnp.concatenate([c, o], axis=-1) @ w_out


def odd_mixer(x, g, pool_w, pool_scale):
    B, S, _ = x.shape
    xn = rmsnorm(x, g).astype(jnp.float32)
    cs = jnp.cumsum(xn, axis=1)
    pos1 = jnp.arange(1, S + 1)
    parts = []
    for gi, w in enumerate(POOL_WINDOWS):
        sl = slice(gi * POOL_DIM, (gi + 1) * POOL_DIM)
        csg = cs[..., sl]
        shifted = jnp.pad(csg, ((0, 0), (w, 0), (0, 0)))[:, :S]
        cnt = jnp.minimum(pos1, w).astype(jnp.float32)
        parts.append((csg - shifted) / cnt[None, :, None] - xn[..., sl])
    p = jnp.stack(parts, axis=2).astype(x.dtype)
    y = jnp.einsum('bsgp,gpo->bsgo', p, pool_w).reshape(B, S, D_MODEL)
    return y * pool_scale


def mlp(x, g, w1, w2):
    h = rmsnorm(x, g) @ w1
    h = jnp.square(jax.nn.relu(h))
    return h @ w2


def setup_inputs(seed: int = 0) -> dict:
    key = jax.random.key(seed)
    ks = jax.random.split(key, 24)
    nrm = lambda k, shape, s: jax.random.normal(k, shape, jnp.float32) * s
    dh = NSA_HEAD_DIM
    return {
        "x": nrm(ks[0], (BATCH, SEQ, D_MODEL), 1.0),
        "mix_norm": 1.0 + nrm(ks[1], (DEPTH, D_MODEL), 0.02),
        "mlp_norm": 1.0 + nrm(ks[2], (DEPTH, D_MODEL), 0.02),
        "w_mlp_in": nrm(ks[3], (DEPTH, D_MODEL, D_FF), D_MODEL ** -0.5),
        "w_mlp_out": nrm(ks[4], (DEPTH, D_FF, D_MODEL), D_FF ** -0.5),
        "w_in": nrm(ks[5], (N_EVEN, D_MODEL, IN_COLS), D_MODEL ** -0.5),
        "w_out": nrm(ks[6], (N_EVEN, D_MODEL, D_MODEL), D_MODEL ** -0.5),
        "conv_w": nrm(ks[7], (N_EVEN, CONV_WIDTH, CONV_DIM), CONV_WIDTH ** -0.5),
        "conv_b": nrm(ks[8], (N_EVEN, CONV_DIM), 0.01),
        "conv_ln_g": 1.0 + nrm(ks[9], (N_EVEN, CONV_DIM), 0.02),
        "conv_ln_b": nrm(ks[10], (N_EVEN, CONV_DIM), 0.01),
        "cmp_pos_k": nrm(ks[11], (N_EVEN, CMP_LEN, dh), 0.1),
        "cmp_pos_v": nrm(ks[12], (N_EVEN, CMP_LEN, dh), 0.1),
        "cmp_k_w1": nrm(ks[13], (N_EVEN, CMP_LEN * dh, CMP_HIDDEN), (CMP_LEN * dh) ** -0.5),
        "cmp_k_w2": nrm(ks[14], (N_EVEN, CMP_HIDDEN, dh), CMP_HIDDEN ** -0.5),
        "cmp_v_w1": nrm(ks[15], (N_EVEN, CMP_LEN * dh, CMP_HIDDEN), (CMP_LEN * dh) ** -0.5),
        "cmp_v_w2": nrm(ks[16], (N_EVEN, CMP_HIDDEN, dh), CMP_HIDDEN ** -0.5),
        "pool_w": nrm(ks[17], (N_ODD, POOL_GROUPS, POOL_DIM, POOL_DIM), POOL_DIM ** -0.5),
        "pool_scale": 1.0 + nrm(ks[18], (N_ODD, D_MODEL), 0.02),
        "final_norm": 1.0 + nrm(ks[19], (D_MODEL,), 0.02),
    }


def reference(x, mix_norm, mlp_norm, w_mlp_in, w_mlp_out, w_in, w_out, conv_w, conv_b,
              conv_ln_g, conv_ln_b, cmp_pos_k, cmp_pos_v, cmp_k_w1, cmp_k_w2,
              cmp_v_w1, cmp_v_w2, pool_w, pool_scale, final_norm):
    for layer in range(DEPTH):
        i = layer // 2
        if layer % 2 == 0:
            x = x + even_mixer(x, mix_norm[layer], w_in[i], w_out[i], conv_w[i], conv_b[i],
                               conv_ln_g[i], conv_ln_b[i], cmp_pos_k[i], cmp_pos_v[i],
                               cmp_k_w1[i], cmp_k_w2[i], cmp_v_w1[i], cmp_v_w2[i])
        else:
            x = x + odd_mixer(x, mix_norm[layer], pool_w[i], pool_scale[i])
        x = x + mlp(x, mlp_norm[layer], w_mlp_in[layer], w_mlp_out[layer])
    return rmsnorm(x, final_norm)
```

```python
import functools

import jax
import jax.numpy as jnp
from jax import lax
from jax.experimental import pallas as pl
from jax.experimental.pallas import tpu as pltpu

F32 = jnp.float32
BF16 = jnp.bfloat16

EPS = 1e-6
NEG_INF = -1e30
CONV_WIDTH = 31
CONV_HALO = 32
NSA_HEADS = 16
NSA_HEAD_DIM = 64
NSA_KV_HEADS = 4
NSA_GROUP = NSA_HEADS // NSA_KV_HEADS
N_BRANCH = 3
CMP_LEN = 32
CMP_STRIDE = 16
SEL_LEN = 64
SEL_TOPK = 16
WINDOW = 512
Q_BLOCK = 128
FORCE_BONUS = 1e3
POOL_WINDOWS = (2, 4, 8, 16)
POOL_HALO = 16
SEL_CHUNK = 256

VMEM_LIMIT = 56 * 1024 * 1024


def _params(*sem):
    return pltpu.CompilerParams(dimension_semantics=sem, vmem_limit_bytes=VMEM_LIMIT)


def _rms(x, g):
    r = lax.rsqrt(jnp.mean(x * x, axis=-1, keepdims=True) + EPS)
    return x * r * g


def _norm_mm_kernel(x_ref, g_ref, w_ref, o_ref, xn_ref, *, act):
    @pl.when(pl.program_id(1) == 0)
    def _():
        xn_ref[...] = _rms(x_ref[...], g_ref[...]).astype(BF16)

    acc = jnp.dot(xn_ref[...], w_ref[...], preferred_element_type=F32)
    if act:
        acc = jnp.square(jnp.maximum(acc, 0.0))
    o_ref[...] = acc.astype(o_ref.dtype)


def norm_matmul(x, g, w, *, out_dtype, act=False, tm=512, tn=512):
    m, k = x.shape
    n = w.shape[1]
    assert m % tm == 0 and n % tn == 0
    return pl.pallas_call(
        functools.partial(_norm_mm_kernel, act=act),
        grid=(m // tm, n // tn),
        in_specs=[
            pl.BlockSpec((tm, k), lambda i, j: (i, 0)),
            pl.BlockSpec((1, k), lambda i, j: (0, 0)),
            pl.BlockSpec((k, tn), lambda i, j: (0, j)),
        ],
        out_specs=pl.BlockSpec((tm, tn), lambda i, j: (i, j)),
        out_shape=jax.ShapeDtypeStruct((m, n), out_dtype),
        scratch_shapes=[pltpu.VMEM((tm, k), BF16)],
        compiler_params=_params("parallel", "arbitrary"),
        name="norm_matmul",
    )(x, g.reshape(1, k), w)


def _mm_res_kernel(h_ref, w_ref, x_ref, g_ref, o_ref, acc_ref, *, final_norm):
    kk = pl.program_id(1)

    @pl.when(kk == 0)
    def _():
        acc_ref[...] = x_ref[...]

    acc_ref[...] += jnp.dot(h_ref[...], w_ref[...], preferred_element_type=F32)

    @pl.when(kk == pl.num_programs(1) - 1)
    def _():
        y = acc_ref[...]
        if final_norm:
            y = _rms(y, g_ref[...])
        o_ref[...] = y


def matmul_residual(h, w, x, g=None, *, tm=512, tk=1024):
    m, k = h.shape
    n = w.shape[1]
    final_norm = g is not None
    if g is None:
        g = jnp.ones((n,), F32)
    return pl.pallas_call(
        functools.partial(_mm_res_kernel, final_norm=final_norm),
        grid=(m // tm, k // tk),
        in_specs=[
            pl.BlockSpec((tm, tk), lambda i, kk: (i, kk)),
            pl.BlockSpec((tk, n), lambda i, kk: (kk, 0)),
            pl.BlockSpec((tm, n), lambda i, kk: (i, 0)),
            pl.BlockSpec((1, n), lambda i, kk: (0, 0)),
        ],
        out_specs=pl.BlockSpec((tm, n), lambda i, kk: (i, 0)),
        out_shape=jax.ShapeDtypeStruct((m, n), F32),
        scratch_shapes=[pltpu.VMEM((tm, n), F32)],
        compiler_params=_params("parallel", "arbitrary"),
        name="matmul_residual",
    )(h, w, x, g.reshape(1, n))


def _conv_kernel(a_ref, gt_ref, w_ref, b_ref, lg_ref, lb_ref, o_ref, cext_ref, *, tt):
    i = pl.program_id(0)

    @pl.when(i == 0)
    def _():
        cext_ref[0:CONV_HALO, :] = jnp.zeros((CONV_HALO, cext_ref.shape[1]), F32)

    @pl.when(i > 0)
    def _():
        cext_ref[0:CONV_HALO, :] = cext_ref[tt:tt + CONV_HALO, :]

    cext_ref[CONV_HALO:, :] = a_ref[...] * jax.nn.sigmoid(gt_ref[...])

    rows = CONV_HALO

    def chunk(j, carry):
        r0 = pl.multiple_of(j * rows, rows)
        ext = cext_ref[pl.ds(r0, rows + CONV_HALO), :]
        acc = jnp.zeros((rows, ext.shape[1]), F32) + b_ref[...]
        for k in range(CONV_WIDTH):
            lo = CONV_HALO - k
            acc = acc + ext[lo:lo + rows, :] * w_ref[CONV_WIDTH - 1 - k:CONV_WIDTH - k, :]
        mu = jnp.mean(acc, axis=-1, keepdims=True)
        xc = acc - mu
        var = jnp.mean(xc * xc, axis=-1, keepdims=True)
        y = xc * lax.rsqrt(var + EPS) * lg_ref[...] + lb_ref[...]
        o_ref[pl.ds(r0, rows), :] = (y * jax.nn.sigmoid(y)).astype(o_ref.dtype)
        return carry

    lax.fori_loop(0, tt // rows, chunk, 0)


def conv_module(proj, conv_w, conv_b, ln_g, ln_b, *, tt=512):
    s = proj.shape[0]
    c = conv_w.shape[1]
    row = lambda v: v.reshape(1, c)
    return pl.pallas_call(
        functools.partial(_conv_kernel, tt=tt),
        grid=(s // tt,),
        in_specs=[
            pl.BlockSpec((tt, c), lambda i: (i, 0)),
            pl.BlockSpec((tt, c), lambda i: (i, 1)),
            pl.BlockSpec((CONV_WIDTH, c), lambda i: (0, 0)),
            pl.BlockSpec((1, c), lambda i: (0, 0)),
            pl.BlockSpec((1, c), lambda i: (0, 0)),
            pl.BlockSpec((1, c), lambda i: (0, 0)),
        ],
        out_specs=pl.BlockSpec((tt, c), lambda i: (i, 0)),
        out_shape=jax.ShapeDtypeStruct((s, c), BF16),
        scratch_shapes=[pltpu.VMEM((tt + CONV_HALO, c), F32)],
        compiler_params=_params("arbitrary"),
        name="conv_module",
    )(proj, proj, conv_w, row(conv_b), row(ln_g), row(ln_b))


def _compress_kernel(kt_ref, pos_ref, w1_ref, w2_ref, o_ref):
    kt = kt_ref[0, 0].astype(F32)
    half = kt.shape[1]
    first = (kt + pos_ref[0, 0:1, :]).astype(BF16)
    second = (kt + pos_ref[0, 1:2, :]).astype(BF16)
    p = jnp.dot(first, w1_ref[0, 0:half, :], preferred_element_type=F32)
    q = jnp.dot(second, w1_ref[0, half:, :], preferred_element_type=F32)
    n = q.shape[0]
    h = p + pltpu.roll(q, n - 1, axis=0)
    h = h * jax.nn.sigmoid(h)
    o_ref[0, 0] = jnp.dot(h.astype(BF16), w2_ref[0], preferred_element_type=F32).astype(o_ref.dtype)


def compress(kt, pos, w1, w2):
    two, g, nch, half = kt.shape
    hid = w1.shape[2]
    dh = w2.shape[2]
    return pl.pallas_call(
        _compress_kernel,
        grid=(two, g),
        in_specs=[
            pl.BlockSpec((1, 1, nch, half), lambda a, b: (a, b, 0, 0)),
            pl.BlockSpec((1, 2, half), lambda a, b: (a, 0, 0)),
            pl.BlockSpec((1, 2 * half, hid), lambda a, b: (a, 0, 0)),
            pl.BlockSpec((1, hid, dh), lambda a, b: (a, 0, 0)),
        ],
        out_specs=pl.BlockSpec((1, 1, nch, dh), lambda a, b: (a, b, 0, 0)),
        out_shape=jax.ShapeDtypeStruct((two, g, nch, dh), BF16),
        compiler_params=_params("parallel", "parallel"),
        name="compress",
    )(kt, pos, w1, w2)


def _split3(x):
    hi = x.astype(BF16)
    r1 = x - hi.astype(F32)
    mid = r1.astype(BF16)
    lo = (r1 - mid.astype(F32)).astype(BF16)
    return hi, mid, lo


def _nsa_kernel(q_ref, kc_ref, vc_ref, ka_ref, va_ref, kw_ref, vwa_ref, gl_ref, o_ref,
                m_ref, acc_ref, *, seq):
    i = pl.program_id(1)
    s0 = i * Q_BLOCK
    dh = NSA_HEAD_DIM
    rows = NSA_GROUP * Q_BLOCK
    n_sel = seq // SEL_LEN
    n_cmp = kc_ref.shape[1]

    q = q_ref[...].reshape(rows, dh) * jnp.asarray(dh ** -0.5, BF16)
    t_row = s0 + (lax.broadcasted_iota(jnp.int32, (rows, 1), 0) & (Q_BLOCK - 1))
    nt = (((1,), (1,)), ((), ()))

    sc = lax.dot_general(q, kc_ref[0], nt, preferred_element_type=F32)
    cmp_end = lax.broadcasted_iota(jnp.int32, (1, n_cmp), 1) * CMP_STRIDE + (CMP_LEN - 1)
    mc = cmp_end <= t_row
    sc = jnp.where(mc, sc, NEG_INF)
    e = jnp.where(mc, jnp.exp(sc - jnp.max(sc, axis=-1, keepdims=True)), 0.0)
    l = jnp.sum(e, axis=-1, keepdims=True)
    pc = e / jnp.where(l > 0.0, l, 1.0)
    oc = jnp.dot(pc.astype(BF16), vc_ref[0], preferred_element_type=F32)

    imp_c = pc[0:Q_BLOCK]
    for r in range(1, NSA_GROUP):
        imp_c = imp_c + pc[r * Q_BLOCK:(r + 1) * Q_BLOCK]
    cn = lax.broadcasted_iota(jnp.int32, (n_cmp, n_sel), 0) * CMP_STRIDE
    sj = lax.broadcasted_iota(jnp.int32, (n_cmp, n_sel), 1) * SEL_LEN
    overlap = jnp.where((cn <= sj + SEL_LEN - 1) & (cn + CMP_LEN - 1 >= sj), 1.0, 0.0).astype(BF16)
    imp = jnp.zeros((Q_BLOCK, n_sel), F32)
    for part in _split3(imp_c):
        imp = imp + jnp.dot(part, overlap, preferred_element_type=F32)

    jj = lax.broadcasted_iota(jnp.int32, (Q_BLOCK, n_sel), 1)
    cur = (s0 + lax.broadcasted_iota(jnp.int32, (Q_BLOCK, 1), 0)) >> (SEL_LEN.bit_length() - 1)
    forced = (jj == 0) | (jj == cur) | (jj == cur - 1)
    future = jj > cur
    work = jnp.where(future, NEG_INF, imp + jnp.where(forced, FORCE_BONUS, 0.0))
    jf = jj.astype(F32)
    chosen = jnp.zeros((Q_BLOCK, n_sel), F32)
    for _ in range(min(SEL_TOPK, n_sel)):
        mx = jnp.max(work, axis=-1, keepdims=True)
        first = jnp.min(jnp.where(work == mx, jf, float(n_sel)), axis=-1, keepdims=True)
        hit = jf == first
        chosen = jnp.where(hit, 1.0, chosen)
        work = jnp.where(hit, -jnp.inf, work)
    bias = jnp.where((chosen > 0.0) & jnp.logical_not(future), 0.0, NEG_INF).astype(BF16)
    q_aug = jnp.concatenate([jnp.concatenate([bias] * NSA_GROUP, axis=0), q], axis=1)

    m_ref[...] = jnp.full(m_ref.shape, NEG_INF, F32)
    acc_ref[...] = jnp.zeros(acc_ref.shape, F32)

    def sweep(c, causal):
        k0 = pl.multiple_of(c * SEL_CHUNK, SEL_CHUNK)
        ka = ka_ref[0, pl.ds(k0, SEL_CHUNK), :]
        s = lax.dot_general(q_aug, ka, nt, preferred_element_type=F32)
        if causal:
            kpos = k0 + lax.broadcasted_iota(jnp.int32, (1, SEL_CHUNK), 1)
            s = jnp.where(kpos <= t_row, s, NEG_INF)
        m_old = m_ref[...]
        m_new = jnp.maximum(m_old, jnp.max(s, axis=-1, keepdims=True))
        p = jnp.exp(s - m_new)
        pv = jnp.dot(p.astype(BF16), va_ref[0, pl.ds(k0, SEL_CHUNK), :], preferred_element_type=F32)
        acc_ref[...] = acc_ref[...] * jnp.exp(m_old - m_new) + pv
        m_ref[...] = m_new

    last = (s0 + Q_BLOCK - 1) >> (SEL_CHUNK.bit_length() - 1)

    def body(c, carry):
        sweep(c, False)
        return carry

    lax.fori_loop(0, last, body, 0)
    sweep(last, True)
    acc = acc_ref[...]
    os_ = acc[:, 0:dh] / acc[:, dh:dh + 1]

    w0 = pl.multiple_of(jnp.maximum(s0 - WINDOW, 0), Q_BLOCK)
    wlen = WINDOW + Q_BLOCK
    sw = lax.dot_general(q, kw_ref[0, pl.ds(w0, wlen), :], nt, preferred_element_type=F32)
    wpos = w0 + lax.broadcasted_iota(jnp.int32, (1, wlen), 1)
    mw = (wpos <= t_row) & (wpos > t_row - WINDOW)
    sw = jnp.where(mw, sw, NEG_INF)
    pw = jnp.exp(sw - jnp.max(sw, axis=-1, keepdims=True))
    accw = jnp.dot(pw.astype(BF16), vwa_ref[0, pl.ds(w0, wlen), :], preferred_element_type=F32)
    ow = accw[:, 0:dh] / accw[:, dh:dh + 1]

    gate = jax.nn.sigmoid(gl_ref[0].astype(F32))
    for r in range(NSA_GROUP):
        sl = slice(r * Q_BLOCK, (r + 1) * Q_BLOCK)
        g0 = gate[:, N_BRANCH * r + 0:N_BRANCH * r + 1]
        g1 = gate[:, N_BRANCH * r + 1:N_BRANCH * r + 2]
        g2 = gate[:, N_BRANCH * r + 2:N_BRANCH * r + 3]
        o_ref[:, r * dh:(r + 1) * dh] = (g0 * oc[sl] + g1 * os_[sl] + g2 * ow[sl]).astype(o_ref.dtype)


def nsa_attention(q_hm, kc, vc, k_aug, v_aug, kw_hm, vw_aug, gl_hm):
    h, s, dh = q_hm.shape
    g = kc.shape[0]
    n_cmp = kc.shape[1]
    rows = NSA_GROUP * Q_BLOCK
    whole = lambda a: pl.BlockSpec((1,) + a.shape[1:], lambda gi, i: (gi, 0, 0))
    return pl.pallas_call(
        functools.partial(_nsa_kernel, seq=s),
        grid=(g, s // Q_BLOCK),
        in_specs=[
            pl.BlockSpec((NSA_GROUP, Q_BLOCK, dh), lambda gi, i: (gi, i, 0)),
            whole(kc), whole(vc), whole(k_aug), whole(v_aug), whole(kw_hm), whole(vw_aug),
            pl.BlockSpec((1, Q_BLOCK, NSA_GROUP * N_BRANCH), lambda gi, i: (gi, i, 0)),
        ],
        out_specs=pl.BlockSpec((Q_BLOCK, NSA_GROUP * dh), lambda gi, i: (i, gi)),
        out_shape=jax.ShapeDtypeStruct((s, h * dh), BF16),
        scratch_shapes=[pltpu.VMEM((rows, 1), F32), pltpu.VMEM((rows, 2 * dh), F32)],
        compiler_params=_params("parallel", "arbitrary"),
        name="nsa_attention",
    )(q_hm, kc, vc, k_aug, v_aug, kw_hm, vw_aug, gl_hm)


def _out_proj_kernel(c_ref, o_ref, w_ref, x_ref, y_ref):
    half = c_ref.shape[1]
    acc = jnp.dot(c_ref[...], w_ref[0:half, :], preferred_element_type=F32)
    acc = acc + jnp.dot(o_ref[...], w_ref[half:, :], preferred_element_type=F32)
    y_ref[...] = x_ref[...] + acc


def out_proj(c, o, w, x, *, tm=512, tn=512):
    m, half = c.shape
    n = w.shape[1]
    return pl.pallas_call(
        _out_proj_kernel,
        grid=(m // tm, n // tn),
        in_specs=[
            pl.BlockSpec((tm, half), lambda i, j: (i, 0)),
            pl.BlockSpec((tm, half), lambda i, j: (i, 0)),
            pl.BlockSpec((2 * half, tn), lambda i, j: (0, j)),
            pl.BlockSpec((tm, tn), lambda i, j: (i, j)),
        ],
        out_specs=pl.BlockSpec((tm, tn), lambda i, j: (i, j)),
        out_shape=jax.ShapeDtypeStruct((m, n), F32),
        compiler_params=_params("parallel", "parallel"),
        name="out_proj",
    )(c, o, w, x)


def _pool_kernel(x_ref, halo_ref, g_ref, w_ref, sc_ref, o_ref, *, tm):
    i = pl.program_id(0)
    x = x_ref[...]
    xn = _rms(x, g_ref[...])
    hn = _rms(halo_ref[...], g_ref[...]) * jnp.where(i > 0, 1.0, 0.0)
    ext = jnp.concatenate([hn, xn], axis=0)
    pd = w_ref.shape[1]
    pos1 = i * tm + 1 + lax.broadcasted_iota(jnp.int32, (tm, 1), 0)
    for gi, win in enumerate(POOL_WINDOWS):
        sl = slice(gi * pd, (gi + 1) * pd)
        s = ext[:, sl]
        sh = 1
        while sh < win:
            s = s + pltpu.roll(s, sh, axis=0)
            sh *= 2
        cnt = jnp.minimum(pos1, win).astype(F32)
        p = s[POOL_HALO:] / cnt - xn[:, sl]
        y = jnp.dot(p.astype(BF16), w_ref[gi], preferred_element_type=F32)
        o_ref[:, sl] = x[:, sl] + y * sc_ref[:, sl]


def pool_mixer(x, g, pool_w, pool_scale, *, tm=512):
    m, d = x.shape
    ng, pd, _ = pool_w.shape
    hb = tm // POOL_HALO
    return pl.pallas_call(
        functools.partial(_pool_kernel, tm=tm),
        grid=(m // tm,),
        in_specs=[
            pl.BlockSpec((tm, d), lambda i: (i, 0)),
            pl.BlockSpec((POOL_HALO, d), lambda i: (jnp.maximum(i * hb - 1, 0), 0)),
            pl.BlockSpec((1, d), lambda i: (0, 0)),
            pl.BlockSpec((ng, pd, pd), lambda i: (0, 0, 0)),
            pl.BlockSpec((1, d), lambda i: (0, 0)),
        ],
        out_specs=pl.BlockSpec((tm, d), lambda i: (i, 0)),
        out_shape=jax.ShapeDtypeStruct((m, d), F32),
        compiler_params=_params("parallel"),
        name="pool_mixer",
    )(x, x, g.reshape(1, d), pool_w, pool_scale.reshape(1, d))


def _mlp(x, g, w1, w2, final_g=None):
    h = norm_matmul(x, g, w1.astype(BF16), out_dtype=BF16, act=True)
    return matmul_residual(h, w2.astype(BF16), x, final_g)


def _even_mixer(x, g, w_in, w_out, conv_w, conv_b, ln_g, ln_b, pos_k, pos_v, kw1, kw2, vw1, vw2):
    s, d = x.shape
    dh, gk, hq = NSA_HEAD_DIM, NSA_KV_HEADS, NSA_HEADS
    conv_cols = 2 * conv_w.shape[1]
    q_cols, kv_cols = hq * dh, gk * dh
    w_in = w_in.astype(BF16)
    w_nsa = w_in[:, conv_cols:]
    pad = (-w_nsa.shape[1]) % 256
    w_nsa = jnp.pad(w_nsa, ((0, 0), (0, pad)))

    proj_conv = norm_matmul(x, g, w_in[:, :conv_cols], out_dtype=F32)
    proj_nsa = norm_matmul(x, g, w_nsa, out_dtype=BF16, tn=256)
    c = conv_module(proj_conv, conv_w, conv_b, ln_g, ln_b)

    cut = lambda k: proj_nsa[:, q_cols + k * kv_cols:q_cols + (k + 1) * kv_cols]
    hm = lambda t, nh: t.reshape(s, nh, dh).transpose(1, 0, 2)
    q_hm = hm(proj_nsa[:, :q_cols], hq)
    k_c, v_c, k_s, v_s, k_w, v_w = (hm(cut(k), gk) for k in range(6))
    gl_hm = proj_nsa[:, q_cols + 6 * kv_cols:q_cols + 6 * kv_cols + hq * N_BRANCH]
    gl_hm = gl_hm.reshape(s, gk, NSA_GROUP * N_BRANCH).transpose(1, 0, 2)

    nch = s // CMP_STRIDE
    kt = jnp.stack([k_c, v_c]).reshape(2, gk, nch, CMP_STRIDE * dh)
    pos = jnp.stack([pos_k, pos_v]).reshape(2, 2, CMP_STRIDE * dh)
    kvc = compress(kt, pos, jnp.stack([kw1, vw1]).astype(BF16), jnp.stack([kw2, vw2]).astype(BF16))

    n_sel = s // SEL_LEN
    onehot = (jnp.arange(s)[:, None] // SEL_LEN == jnp.arange(n_sel)[None, :]).astype(BF16)
    k_aug = jnp.concatenate([jnp.broadcast_to(onehot, (gk, s, n_sel)), k_s], axis=-1)
    ones = jnp.ones((gk, s, dh), BF16)
    v_aug = jnp.concatenate([v_s, ones], axis=-1)
    vw_aug = jnp.concatenate([v_w, ones], axis=-1)
    o = nsa_attention(q_hm, kvc[0], kvc[1], k_aug, v_aug, k_w, vw_aug, gl_hm)
    return out_proj(c, o, w_out.astype(BF16), x)


def kernel(x, mix_norm, mlp_norm, w_mlp_in, w_mlp_out, w_in, w_out, conv_w, conv_b, conv_ln_g,
           conv_ln_b, cmp_pos_k, cmp_pos_v, cmp_k_w1, cmp_k_w2, cmp_v_w1, cmp_v_w2, pool_w,
           pool_scale, final_norm):
    b, s, d = x.shape
    depth = mix_norm.shape[0]
    outs = []
    for bi in range(b):
        h = x[bi]
        for layer in range(depth):
            i = layer // 2
            if layer % 2 == 0:
                h = _even_mixer(h, mix_norm[layer], w_in[i], w_out[i], conv_w[i], conv_b[i],
                                conv_ln_g[i], conv_ln_b[i], cmp_pos_k[i], cmp_pos_v[i],
                                cmp_k_w1[i], cmp_k_w2[i], cmp_v_w1[i], cmp_v_w2[i])
            else:
                h = pool_mixer(h, mix_norm[layer], pool_w[i].astype(BF16), pool_scale[i])
            fg = final_norm if layer == depth - 1 else None
            h = _mlp(h, mlp_norm[layer], w_mlp_in[layer], w_mlp_out[layer], fg)
        outs.append(h)
    return jnp.stack(outs)
```

```python
import functools

import jax
import jax.numpy as jnp
from jax import lax
from jax.experimental import pallas as pl
from jax.experimental.pallas import tpu as pltpu

F32 = jnp.float32
BF16 = jnp.bfloat16

EPS = 1e-6
NEG_INF = -1e30
CONV_WIDTH = 31
CONV_HALO = 32
NSA_HEADS = 16
NSA_HEAD_DIM = 64
NSA_KV_HEADS = 4
NSA_GROUP = NSA_HEADS // NSA_KV_HEADS
N_BRANCH = 3
CMP_LEN = 32
CMP_STRIDE = 16
SEL_LEN = 64
SEL_TOPK = 16
WINDOW = 512
Q_BLOCK = 128
FORCE_BONUS = 1e3
POOL_WINDOWS = (2, 4, 8, 16)
POOL_HALO = 16
SEL_CHUNK = 512
ONES_ROWS = 16

VMEM_LIMIT = 56 * 1024 * 1024


def _params(*sem):
    return pltpu.CompilerParams(dimension_semantics=sem, vmem_limit_bytes=VMEM_LIMIT)


def _rms(x, g):
    r = lax.rsqrt(jnp.mean(x * x, axis=-1, keepdims=True) + EPS)
    return x * r * g


def _norm_mm_kernel(x_ref, g_ref, w_ref, o_ref, xn_ref, *, act):
    @pl.when(pl.program_id(1) == 0)
    def _():
        xn_ref[...] = _rms(x_ref[...], g_ref[...]).astype(BF16)

    acc = jnp.dot(xn_ref[...], w_ref[...], preferred_element_type=F32)
    if act:
        acc = jnp.square(jnp.maximum(acc, 0.0))
    o_ref[...] = acc.astype(o_ref.dtype)


def norm_matmul(x, g, w, *, out_dtype, act=False, tm=512, tn=512):
    m, k = x.shape
    n = w.shape[1]
    assert m % tm == 0 and n % tn == 0
    return pl.pallas_call(
        functools.partial(_norm_mm_kernel, act=act),
        grid=(m // tm, n // tn),
        in_specs=[
            pl.BlockSpec((tm, k), lambda i, j: (i, 0)),
            pl.BlockSpec((1, k), lambda i, j: (0, 0)),
            pl.BlockSpec((k, tn), lambda i, j: (0, j)),
        ],
        out_specs=pl.BlockSpec((tm, tn), lambda i, j: (i, j)),
        out_shape=jax.ShapeDtypeStruct((m, n), out_dtype),
        scratch_shapes=[pltpu.VMEM((tm, k), BF16)],
        compiler_params=_params("parallel", "arbitrary"),
        name="norm_matmul",
    )(x, g.reshape(1, k), w)


def _mm_res_kernel(h_ref, w_ref, x_ref, g_ref, o_ref, acc_ref, *, final_norm):
    kk = pl.program_id(1)

    @pl.when(kk == 0)
    def _():
        acc_ref[...] = x_ref[...]

    acc_ref[...] += jnp.dot(h_ref[...], w_ref[...], preferred_element_type=F32)

    @pl.when(kk == pl.num_programs(1) - 1)
    def _():
        y = acc_ref[...]
        if final_norm:
            y = _rms(y, g_ref[...])
        o_ref[...] = y


def matmul_residual(h, w, x, g=None, *, tm=512, tk=1024):
    m, k = h.shape
    n = w.shape[1]
    final_norm = g is not None
    if g is None:
        g = jnp.ones((n,), F32)
    return pl.pallas_call(
        functools.partial(_mm_res_kernel, final_norm=final_norm),
        grid=(m // tm, k // tk),
        in_specs=[
            pl.BlockSpec((tm, tk), lambda i, kk: (i, kk)),
            pl.BlockSpec((tk, n), lambda i, kk: (kk, 0)),
            pl.BlockSpec((tm, n), lambda i, kk: (i, 0)),
            pl.BlockSpec((1, n), lambda i, kk: (0, 0)),
        ],
        out_specs=pl.BlockSpec((tm, n), lambda i, kk: (i, 0)),
        out_shape=jax.ShapeDtypeStruct((m, n), F32),
        scratch_shapes=[pltpu.VMEM((tm, n), F32)],
        compiler_params=_params("parallel", "arbitrary"),
        name="matmul_residual",
    )(h, w, x, g.reshape(1, n))


def _conv_kernel(a_ref, gt_ref, w_ref, b_ref, lg_ref, lb_ref, o_ref, cext_ref, *, tt):
    i = pl.program_id(0)

    @pl.when(i == 0)
    def _():
        cext_ref[0:CONV_HALO, :] = jnp.zeros((CONV_HALO, cext_ref.shape[1]), F32)

    @pl.when(i > 0)
    def _():
        cext_ref[0:CONV_HALO, :] = cext_ref[tt:tt + CONV_HALO, :]

    cext_ref[CONV_HALO:, :] = a_ref[...] * jax.nn.sigmoid(gt_ref[...])

    rows = CONV_HALO

    def chunk(j, carry):
        r0 = pl.multiple_of(j * rows, rows)
        ext = cext_ref[pl.ds(r0, rows + CONV_HALO), :]
        acc = jnp.zeros((rows, ext.shape[1]), F32) + b_ref[...]
        for k in range(CONV_WIDTH):
            lo = CONV_HALO - k
            acc = acc + ext[lo:lo + rows, :] * w_ref[CONV_WIDTH - 1 - k:CONV_WIDTH - k, :]
        mu = jnp.mean(acc, axis=-1, keepdims=True)
        xc = acc - mu
        var = jnp.mean(xc * xc, axis=-1, keepdims=True)
        y = xc * lax.rsqrt(var + EPS) * lg_ref[...] + lb_ref[...]
        o_ref[pl.ds(r0, rows), :] = (y * jax.nn.sigmoid(y)).astype(o_ref.dtype)
        return carry

    lax.fori_loop(0, tt // rows, chunk, 0)


def conv_module(proj, conv_w, conv_b, ln_g, ln_b, *, tt=512):
    s = proj.shape[0]
    c = conv_w.shape[1]
    row = lambda v: v.reshape(1, c)
    return pl.pallas_call(
        functools.partial(_conv_kernel, tt=tt),
        grid=(s // tt,),
        in_specs=[
            pl.BlockSpec((tt, c), lambda i: (i, 0)),
            pl.BlockSpec((tt, c), lambda i: (i, 1)),
            pl.BlockSpec((CONV_WIDTH, c), lambda i: (0, 0)),
            pl.BlockSpec((1, c), lambda i: (0, 0)),
            pl.BlockSpec((1, c), lambda i: (0, 0)),
            pl.BlockSpec((1, c), lambda i: (0, 0)),
        ],
        out_specs=pl.BlockSpec((tt, c), lambda i: (i, 0)),
        out_shape=jax.ShapeDtypeStruct((s, c), BF16),
        scratch_shapes=[pltpu.VMEM((tt + CONV_HALO, c), F32)],
        compiler_params=_params("arbitrary"),
        name="conv_module",
    )(proj, proj, conv_w, row(conv_b), row(ln_g), row(ln_b))


def _compress_kernel(kt_ref, pos_ref, w1_ref, w2_ref, o_ref):
    kt = kt_ref[0, 0].astype(F32)
    half = kt.shape[1]
    first = (kt + pos_ref[0, 0:1, :]).astype(BF16)
    second = (kt + pos_ref[0, 1:2, :]).astype(BF16)
    p = jnp.dot(first, w1_ref[0, 0:half, :], preferred_element_type=F32)
    q = jnp.dot(second, w1_ref[0, half:, :], preferred_element_type=F32)
    n = q.shape[0]
    h = p + pltpu.roll(q, n - 1, axis=0)
    h = h * jax.nn.sigmoid(h)
    o_ref[0, 0] = jnp.dot(h.astype(BF16), w2_ref[0], preferred_element_type=F32).astype(o_ref.dtype)


def compress(kt, pos, w1, w2):
    two, g, nch, half = kt.shape
    hid = w1.shape[2]
    dh = w2.shape[2]
    return pl.pallas_call(
        _compress_kernel,
        grid=(two, g),
        in_specs=[
            pl.BlockSpec((1, 1, nch, half), lambda a, b: (a, b, 0, 0)),
            pl.BlockSpec((1, 2, half), lambda a, b: (a, 0, 0)),
            pl.BlockSpec((1, 2 * half, hid), lambda a, b: (a, 0, 0)),
            pl.BlockSpec((1, hid, dh), lambda a, b: (a, 0, 0)),
        ],
        out_specs=pl.BlockSpec((1, 1, nch, dh), lambda a, b: (a, b, 0, 0)),
        out_shape=jax.ShapeDtypeStruct((two, g, nch, dh), BF16),
        compiler_params=_params("parallel", "parallel"),
        name="compress",
    )(kt, pos, w1, w2)


def _split3(x):
    hi = x.astype(BF16)
    r1 = x - hi.astype(F32)
    mid = r1.astype(BF16)
    lo = (r1 - mid.astype(F32)).astype(BF16)
    return hi, mid, lo


def _nsa_kernel(qt_ref, kc_ref, vct_ref, ka_ref, vat_ref, kw_ref, vwt_ref, glt_ref, o_ref,
                qa_ref, m_ref, acc_ref, s_ref, part_ref, *, seq):
    i = pl.program_id(1)
    s0 = i * Q_BLOCK
    dh = NSA_HEAD_DIM
    cols = NSA_GROUP * Q_BLOCK
    pair = 2 * Q_BLOCK
    n_sel = seq // SEL_LEN
    n_cmp = kc_ref.shape[1]

    qt = jnp.concatenate([qt_ref[r] for r in range(NSA_GROUP)], axis=1) * jnp.asarray(dh ** -0.5, BF16)
    t_col = s0 + (lax.broadcasted_iota(jnp.int32, (1, cols), 1) & (Q_BLOCK - 1))

    sc = jnp.dot(kc_ref[0], qt, preferred_element_type=F32)
    cmp_end = lax.broadcasted_iota(jnp.int32, (n_cmp, 1), 0) * CMP_STRIDE + (CMP_LEN - 1)
    mc = cmp_end <= t_col
    sc = jnp.where(mc, sc, NEG_INF)
    e = jnp.where(mc, jnp.exp(sc - jnp.max(sc, axis=0, keepdims=True)), 0.0)
    l = jnp.sum(e, axis=0, keepdims=True)
    pc = e * (1.0 / jnp.where(l > 0.0, l, 1.0))
    oc = jnp.dot(vct_ref[0], pc.astype(BF16), preferred_element_type=F32)

    wlen = WINDOW + Q_BLOCK
    w0 = pl.multiple_of(jnp.maximum(s0 - WINDOW, 0), Q_BLOCK)
    wc = jnp.maximum(i - WINDOW // Q_BLOCK, 0)
    sw = jnp.dot(kw_ref[0, pl.ds(w0, wlen), :], qt, preferred_element_type=F32)
    wpos = w0 + lax.broadcasted_iota(jnp.int32, (wlen, 1), 0)
    mw = (wpos <= t_col) & (wpos > t_col - WINDOW)
    sw = jnp.where(mw, sw, NEG_INF)
    pw = jnp.exp(sw - jnp.max(sw, axis=0, keepdims=True)).astype(BF16)
    accw = jnp.zeros((vwt_ref.shape[2], cols), F32)
    for j in range(wlen // Q_BLOCK):
        accw = accw + jnp.dot(vwt_ref[0, wc + j], pw[j * Q_BLOCK:(j + 1) * Q_BLOCK],
                              preferred_element_type=F32)
    ow = accw[0:dh] * (1.0 / accw[dh:dh + 1])

    def gate_row(b):
        gate = jax.nn.sigmoid(glt_ref[0].astype(F32))
        return jnp.concatenate(
            [gate[N_BRANCH * r + b:N_BRANCH * r + b + 1] for r in range(NSA_GROUP)], axis=1)

    part_ref[...] = gate_row(0) * oc + gate_row(2) * ow

    imp_c = pc[:, 0:Q_BLOCK]
    for r in range(1, NSA_GROUP):
        imp_c = imp_c + pc[:, r * Q_BLOCK:(r + 1) * Q_BLOCK]
    sj = lax.broadcasted_iota(jnp.int32, (n_sel, n_cmp), 0) * SEL_LEN
    cn = lax.broadcasted_iota(jnp.int32, (n_sel, n_cmp), 1) * CMP_STRIDE
    overlap = jnp.where((cn <= sj + SEL_LEN - 1) & (cn + CMP_LEN - 1 >= sj), 1.0, 0.0).astype(BF16)
    imp = jnp.zeros((n_sel, Q_BLOCK), F32)
    for part in _split3(imp_c):
        imp = imp + jnp.dot(overlap, part, preferred_element_type=F32)

    jj = lax.broadcasted_iota(jnp.int32, (n_sel, Q_BLOCK), 0)
    cur = (s0 + lax.broadcasted_iota(jnp.int32, (1, Q_BLOCK), 1)) >> (SEL_LEN.bit_length() - 1)
    forced = (jj == 0) | (jj == cur) | (jj == cur - 1)
    future = jj > cur
    work = jnp.where(future, NEG_INF, imp + jnp.where(forced, FORCE_BONUS, 0.0))
    jf = jj.astype(F32)
    chosen = jnp.zeros((n_sel, Q_BLOCK), F32)
    for _ in range(min(SEL_TOPK, n_sel)):
        mx = jnp.max(work, axis=0, keepdims=True)
        first = jnp.min(jnp.where(work == mx, jf, float(n_sel)), axis=0, keepdims=True)
        hit = jf == first
        chosen = jnp.where(hit, 1.0, chosen)
        work = jnp.where(hit, -jnp.inf, work)
    bias = jnp.where((chosen > 0.0) & jnp.logical_not(future), 0.0, NEG_INF).astype(BF16)
    qa_ref[0:n_sel, :] = jnp.concatenate([bias] * NSA_GROUP, axis=1)
    qa_ref[n_sel:, :] = qt

    m_ref[...] = jnp.full(m_ref.shape, NEG_INF, F32)
    acc_ref[...] = jnp.zeros(acc_ref.shape, F32)

    def scores(c, dst_ref):
        k0 = pl.multiple_of(c * SEL_CHUNK, SEL_CHUNK)
        ka = ka_ref[0, pl.ds(k0, SEL_CHUNK), :]
        for h0 in range(0, cols, pair):
            dst_ref[:, h0:h0 + pair] = jnp.dot(ka, qa_ref[:, h0:h0 + pair], preferred_element_type=F32)

    def accumulate(c, src_ref, causal):
        s = src_ref[...]
        if causal:
            kpos = c * SEL_CHUNK + lax.broadcasted_iota(jnp.int32, (SEL_CHUNK, 1), 0)
            s = jnp.where(kpos <= t_col, s, NEG_INF)
        m_old = m_ref[...]
        m_new = jnp.maximum(m_old, jnp.max(s, axis=0, keepdims=True))
        p = jnp.exp(s - m_new).astype(BF16)
        pv = jnp.dot(vat_ref[0, c], p, preferred_element_type=F32)
        acc_ref[...] = acc_ref[...] * jnp.exp(m_old - m_new) + pv
        m_ref[...] = m_new

    n_full = s0 >> (SEL_CHUNK.bit_length() - 1)
    sa_ref, sb_ref = s_ref.at[0], s_ref.at[1]
    scores(0, sa_ref)

    def body(j, carry):
        c = 2 * j
        scores(c + 1, sb_ref)
        accumulate(c, sa_ref, False)
        scores(c + 2, sa_ref)
        accumulate(c + 1, sb_ref, False)
        return carry

    lax.fori_loop(0, n_full >> 1, body, 0)

    @pl.when((n_full & 1) == 1)
    def _():
        scores(n_full, sb_ref)
        accumulate(n_full - 1, sa_ref, False)

    accumulate(n_full, s_ref.at[n_full & 1], True)
    acc = acc_ref[...]
    os_ = acc[0:dh] * (1.0 / acc[dh:dh + 1])

    out_t = part_ref[...] + gate_row(1) * os_
    out_t = jnp.concatenate([out_t[:, r * Q_BLOCK:(r + 1) * Q_BLOCK] for r in range(NSA_GROUP)], axis=0)
    o_ref[...] = out_t.T.astype(o_ref.dtype)


def nsa_attention(qt, kc, vct, k_aug, vat, kw, vwt, glt):
    h, dh, s = qt.shape
    g = kc.shape[0]
    dv = vat.shape[2]
    cols = NSA_GROUP * Q_BLOCK
    whole = lambda a: pl.BlockSpec((1,) + a.shape[1:], lambda gi, i: (gi,) + (0,) * (a.ndim - 1))
    return pl.pallas_call(
        functools.partial(_nsa_kernel, seq=s),
        grid=(g, s // Q_BLOCK),
        in_specs=[
            pl.BlockSpec((NSA_GROUP, dh, Q_BLOCK), lambda gi, i: (gi, 0, i)),
            whole(kc), whole(vct), whole(k_aug), whole(vat), whole(kw), whole(vwt),
            pl.BlockSpec((1, NSA_GROUP * N_BRANCH, Q_BLOCK), lambda gi, i: (gi, 0, i)),
        ],
        out_specs=pl.BlockSpec((Q_BLOCK, NSA_GROUP * dh), lambda gi, i: (i, gi)),
        out_shape=jax.ShapeDtypeStruct((s, h * dh), BF16),
        scratch_shapes=[pltpu.VMEM((k_aug.shape[2], cols), BF16),
                        pltpu.VMEM((1, cols), F32),
                        pltpu.VMEM((dv, cols), F32),
                        pltpu.VMEM((2, SEL_CHUNK, cols), F32),
                        pltpu.VMEM((dh, cols), F32)],
        compiler_params=_params("parallel", "arbitrary"),
        name="nsa_attention",
    )(qt, kc, vct, k_aug, vat, kw, vwt, glt)


def _out_proj_kernel(c_ref, o_ref, w_ref, x_ref, y_ref):
    half = c_ref.shape[1]
    acc = jnp.dot(c_ref[...], w_ref[0:half, :], preferred_element_type=F32)
    acc = acc + jnp.dot(o_ref[...], w_ref[half:, :], preferred_element_type=F32)
    y_ref[...] = x_ref[...] + acc


def out_proj(c, o, w, x, *, tm=512, tn=512):
    m, half = c.shape
    n = w.shape[1]
    return pl.pallas_call(
        _out_proj_kernel,
        grid=(m // tm, n // tn),
        in_specs=[
            pl.BlockSpec((tm, half), lambda i, j: (i, 0)),
            pl.BlockSpec((tm, half), lambda i, j: (i, 0)),
            pl.BlockSpec((2 * half, tn), lambda i, j: (0, j)),
            pl.BlockSpec((tm, tn), lambda i, j: (i, j)),
        ],
        out_specs=pl.BlockSpec((tm, tn), lambda i, j: (i, j)),
        out_shape=jax.ShapeDtypeStruct((m, n), F32),
        compiler_params=_params("parallel", "parallel"),
        name="out_proj",
    )(c, o, w, x)


def _pool_kernel(x_ref, halo_ref, g_ref, w_ref, sc_ref, o_ref, *, tm):
    i = pl.program_id(0)
    x = x_ref[...]
    xn = _rms(x, g_ref[...])
    hn = _rms(halo_ref[...], g_ref[...]) * jnp.where(i > 0, 1.0, 0.0)
    ext = jnp.concatenate([hn, xn], axis=0)
    pd = w_ref.shape[1]
    pos1 = i * tm + 1 + lax.broadcasted_iota(jnp.int32, (tm, 1), 0)
    for gi, win in enumerate(POOL_WINDOWS):
        sl = slice(gi * pd, (gi + 1) * pd)
        s = ext[:, sl]
        sh = 1
        while sh < win:
            s = s + pltpu.roll(s, sh, axis=0)
            sh *= 2
        cnt = jnp.minimum(pos1, win).astype(F32)
        p = s[POOL_HALO:] / cnt - xn[:, sl]
        y = jnp.dot(p.astype(BF16), w_ref[gi], preferred_element_type=F32)
        o_ref[:, sl] = x[:, sl] + y * sc_ref[:, sl]


def pool_mixer(x, g, pool_w, pool_scale, *, tm=512):
    m, d = x.shape
    ng, pd, _ = pool_w.shape
    hb = tm // POOL_HALO
    return pl.pallas_call(
        functools.partial(_pool_kernel, tm=tm),
        grid=(m // tm,),
        in_specs=[
            pl.BlockSpec((tm, d), lambda i: (i, 0)),
            pl.BlockSpec((POOL_HALO, d), lambda i: (jnp.maximum(i * hb - 1, 0), 0)),
            pl.BlockSpec((1, d), lambda i: (0, 0)),
            pl.BlockSpec((ng, pd, pd), lambda i: (0, 0, 0)),
            pl.BlockSpec((1, d), lambda i: (0, 0)),
        ],
        out_specs=pl.BlockSpec((tm, d), lambda i: (i, 0)),
        out_shape=jax.ShapeDtypeStruct((m, d), F32),
        compiler_params=_params("parallel"),
        name="pool_mixer",
    )(x, x, g.reshape(1, d), pool_w, pool_scale.reshape(1, d))


def _mlp(x, g, w1, w2, final_g=None):
    h = norm_matmul(x, g, w1.astype(BF16), out_dtype=BF16, act=True)
    return matmul_residual(h, w2.astype(BF16), x, final_g)


def _even_mixer(x, g, w_in, w_out, conv_w, conv_b, ln_g, ln_b, pos_k, pos_v, kw1, kw2, vw1, vw2):
    s, d = x.shape
    dh, gk, hq = NSA_HEAD_DIM, NSA_KV_HEADS, NSA_HEADS
    conv_cols = 2 * conv_w.shape[1]
    q_cols, kv_cols = hq * dh, gk * dh
    w_in = w_in.astype(BF16)
    w_nsa = w_in[:, conv_cols:]
    pad = (-w_nsa.shape[1]) % 256
    w_nsa = jnp.pad(w_nsa, ((0, 0), (0, pad)))

    proj_conv = norm_matmul(x, g, w_in[:, :conv_cols], out_dtype=F32)
    proj_nsa = norm_matmul(x, g, w_nsa, out_dtype=BF16, tn=256)
    c = conv_module(proj_conv, conv_w, conv_b, ln_g, ln_b)

    cut = lambda k: proj_nsa[:, q_cols + k * kv_cols:q_cols + (k + 1) * kv_cols]
    hm = lambda t: t.reshape(s, gk, dh).transpose(1, 0, 2)
    qt = proj_nsa[:, :q_cols].reshape(s, hq, dh).transpose(1, 2, 0)
    k_c, v_c, k_s, v_s, k_w, v_w = (hm(cut(k)) for k in range(6))
    glt = proj_nsa[:, q_cols + 6 * kv_cols:q_cols + 6 * kv_cols + hq * N_BRANCH]
    glt = glt.reshape(s, gk, NSA_GROUP * N_BRANCH).transpose(1, 2, 0)

    nch = s // CMP_STRIDE
    kt = jnp.stack([k_c, v_c]).reshape(2, gk, nch, CMP_STRIDE * dh)
    pos = jnp.stack([pos_k, pos_v]).reshape(2, 2, CMP_STRIDE * dh)
    kvc = compress(kt, pos, jnp.stack([kw1, vw1]).astype(BF16), jnp.stack([kw2, vw2]).astype(BF16))

    n_sel = s // SEL_LEN
    onehot = (jnp.arange(s)[:, None] // SEL_LEN == jnp.arange(n_sel)[None, :]).astype(BF16)
    k_aug = jnp.concatenate([jnp.broadcast_to(onehot, (gk, s, n_sel)), k_s], axis=-1)
    ones = jnp.ones((gk, s, ONES_ROWS), BF16)
    chunk_t = lambda v, n: (jnp.concatenate([v, ones], axis=-1)
                            .reshape(gk, s // n, n, dh + ONES_ROWS).transpose(0, 1, 3, 2))
    o = nsa_attention(qt, kvc[0], kvc[1].transpose(0, 2, 1), k_aug, chunk_t(v_s, SEL_CHUNK),
                      k_w, chunk_t(v_w, Q_BLOCK), glt)
    return out_proj(c, o, w_out.astype(BF16), x)


def kernel(x, mix_norm, mlp_norm, w_mlp_in, w_mlp_out, w_in, w_out, conv_w, conv_b, conv_ln_g,
           conv_ln_b, cmp_pos_k, cmp_pos_v, cmp_k_w1, cmp_k_w2, cmp_v_w1, cmp_v_w2, pool_w,
           pool_scale, final_norm):
    b, s, d = x.shape
    depth = mix_norm.shape[0]
    outs = []
    for bi in range(b):
        h = x[bi]
        for layer in range(depth):
            i = layer // 2
            if layer % 2 == 0:
                h = _even_mixer(h, mix_norm[layer], w_in[i], w_out[i], conv_w[i], conv_b[i],
                                conv_ln_g[i], conv_ln_b[i], cmp_pos_k[i], cmp_pos_v[i],
                                cmp_k_w1[i], cmp_k_w2[i], cmp_v_w1[i], cmp_v_w2[i])
            else:
                h = pool_mixer(h, mix_norm[layer], pool_w[i].astype(BF16), pool_scale[i])
            fg = final_norm if layer == depth - 1 else None
            h = _mlp(h, mlp_norm[layer], w_mlp_in[layer], w_mlp_out[layer], fg)
        outs.append(h)
    return jnp.stack(outs)
```

```python
import functools

import jax
import jax.numpy as jnp
from jax import lax
from jax.experimental import pallas as pl
from jax.experimental.pallas import tpu as pltpu

F32 = jnp.float32
BF16 = jnp.bfloat16

EPS = 1e-6
NEG_INF = -1e30
CONV_WIDTH = 31
CONV_HALO = 32
NSA_HEADS = 16
NSA_HEAD_DIM = 64
NSA_KV_HEADS = 4
NSA_GROUP = NSA_HEADS // NSA_KV_HEADS
N_BRANCH = 3
CMP_LEN = 32
CMP_STRIDE = 16
SEL_LEN = 64
SEL_TOPK = 16
WINDOW = 512
Q_BLOCK = 128
FORCE_BONUS = 1e3
POOL_WINDOWS = (2, 4, 8, 16)
POOL_HALO = 16
SEL_CHUNK = 512
ONES_ROWS = 16

VMEM_LIMIT = 56 * 1024 * 1024


def _params(*sem):
    return pltpu.CompilerParams(dimension_semantics=sem, vmem_limit_bytes=VMEM_LIMIT)


def _rms(x, g):
    r = lax.rsqrt(jnp.mean(x * x, axis=-1, keepdims=True) + EPS)
    return x * r * g


def _norm_mm_kernel(x_ref, g_ref, w_ref, o_ref, xn_ref, *, act):
    @pl.when(pl.program_id(1) == 0)
    def _():
        xn_ref[...] = _rms(x_ref[...], g_ref[...]).astype(BF16)

    acc = jnp.dot(xn_ref[...], w_ref[...], preferred_element_type=F32)
    if act:
        acc = jnp.square(jnp.maximum(acc, 0.0))
    o_ref[...] = acc.astype(o_ref.dtype)


def norm_matmul(x, g, w, *, out_dtype, act=False, tm=1024, tn=512):
    m, k = x.shape
    n = w.shape[1]
    assert m % tm == 0 and n % tn == 0
    return pl.pallas_call(
        functools.partial(_norm_mm_kernel, act=act),
        grid=(m // tm, n // tn),
        in_specs=[
            pl.BlockSpec((tm, k), lambda i, j: (i, 0)),
            pl.BlockSpec((1, k), lambda i, j: (0, 0)),
            pl.BlockSpec((k, tn), lambda i, j: (0, j)),
        ],
        out_specs=pl.BlockSpec((tm, tn), lambda i, j: (i, j)),
        out_shape=jax.ShapeDtypeStruct((m, n), out_dtype),
        scratch_shapes=[pltpu.VMEM((tm, k), BF16)],
        compiler_params=_params("parallel", "arbitrary"),
        name="norm_matmul",
    )(x, g.reshape(1, k), w)


def _mlp_kernel(x_ref, g_ref, w1_ref, w2_ref, gf_ref, o_ref, xn_ref, h_ref, *, final_norm):
    f = pl.program_id(1)

    @pl.when(f == 0)
    def _():
        x = x_ref[...]
        xn_ref[...] = _rms(x, g_ref[...]).astype(BF16)
        o_ref[...] = x
        h_ref[...] = jnp.zeros(h_ref.shape, BF16)

    h_new = jnp.dot(xn_ref[...], w1_ref[...].astype(BF16), preferred_element_type=F32)
    o_ref[...] += jnp.dot(h_ref[...], w2_ref[...].astype(BF16), preferred_element_type=F32)
    h_ref[...] = jnp.square(jnp.maximum(h_new, 0.0)).astype(BF16)

    if final_norm:
        @pl.when(f == pl.num_programs(1) - 1)
        def _():
            o_ref[...] = _rms(o_ref[...], gf_ref[...])


def fused_mlp(x, g, w1, w2, gf=None, *, tm=2048, tf=256):
    m, d = x.shape
    nf = w1.shape[1] // tf
    final_norm = gf is not None
    if gf is None:
        gf = g
    once = pl.Buffered(1)
    return pl.pallas_call(
        functools.partial(_mlp_kernel, final_norm=final_norm),
        grid=(m // tm, nf + 1),
        in_specs=[
            pl.BlockSpec((tm, d), lambda i, f: (i, 0), pipeline_mode=once),
            pl.BlockSpec((1, d), lambda i, f: (0, 0)),
            pl.BlockSpec((d, tf), lambda i, f: (0, jnp.minimum(f, nf - 1))),
            pl.BlockSpec((tf, d), lambda i, f: (jnp.maximum(f - 1, 0), 0)),
            pl.BlockSpec((1, d), lambda i, f: (0, 0)),
        ],
        out_specs=pl.BlockSpec((tm, d), lambda i, f: (i, 0), pipeline_mode=once),
        out_shape=jax.ShapeDtypeStruct((m, d), F32),
        scratch_shapes=[pltpu.VMEM((tm, d), BF16), pltpu.VMEM((tm, tf), BF16)],
        compiler_params=_params("parallel", "arbitrary"),
        name="fused_mlp",
    )(x, g.reshape(1, d), w1, w2, gf.reshape(1, d))


def _conv_kernel(a_ref, gt_ref, w_ref, b_ref, lg_ref, lb_ref, o_ref, cext_ref, *, tt):
    i = pl.program_id(0)

    @pl.when(i == 0)
    def _():
        cext_ref[0:CONV_HALO, :] = jnp.zeros((CONV_HALO, cext_ref.shape[1]), F32)

    @pl.when(i > 0)
    def _():
        cext_ref[0:CONV_HALO, :] = cext_ref[tt:tt + CONV_HALO, :]

    cext_ref[CONV_HALO:, :] = a_ref[...] * jax.nn.sigmoid(gt_ref[...])

    rows = CONV_HALO

    def chunk(j, carry):
        r0 = pl.multiple_of(j * rows, rows)
        ext = cext_ref[pl.ds(r0, rows + CONV_HALO), :]
        acc = jnp.zeros((rows, ext.shape[1]), F32) + b_ref[...]
        for k in range(CONV_WIDTH):
            lo = CONV_HALO - k
            acc = acc + ext[lo:lo + rows, :] * w_ref[CONV_WIDTH - 1 - k:CONV_WIDTH - k, :]
        mu = jnp.mean(acc, axis=-1, keepdims=True)
        xc = acc - mu
        var = jnp.mean(xc * xc, axis=-1, keepdims=True)
        y = xc * lax.rsqrt(var + EPS) * lg_ref[...] + lb_ref[...]
        o_ref[pl.ds(r0, rows), :] = (y * jax.nn.sigmoid(y)).astype(o_ref.dtype)
        return carry

    lax.fori_loop(0, tt // rows, chunk, 0)


def conv_module(proj, conv_w, conv_b, ln_g, ln_b, *, tt=512):
    s = proj.shape[0]
    c = conv_w.shape[1]
    row = lambda v: v.reshape(1, c)
    return pl.pallas_call(
        functools.partial(_conv_kernel, tt=tt),
        grid=(s // tt,),
        in_specs=[
            pl.BlockSpec((tt, c), lambda i: (i, 0)),
            pl.BlockSpec((tt, c), lambda i: (i, 1)),
            pl.BlockSpec((CONV_WIDTH, c), lambda i: (0, 0)),
            pl.BlockSpec((1, c), lambda i: (0, 0)),
            pl.BlockSpec((1, c), lambda i: (0, 0)),
            pl.BlockSpec((1, c), lambda i: (0, 0)),
        ],
        out_specs=pl.BlockSpec((tt, c), lambda i: (i, 0)),
        out_shape=jax.ShapeDtypeStruct((s, c), BF16),
        scratch_shapes=[pltpu.VMEM((tt + CONV_HALO, c), F32)],
        compiler_params=_params("arbitrary"),
        name="conv_module",
    )(proj, proj, conv_w, row(conv_b), row(ln_g), row(ln_b))


def _compress_kernel(kt_ref, pos_ref, w1_ref, w2_ref, o_ref):
    kt = kt_ref[0, 0].astype(F32)
    half = kt.shape[1]
    first = (kt + pos_ref[0, 0:1, :]).astype(BF16)
    second = (kt + pos_ref[0, 1:2, :]).astype(BF16)
    p = jnp.dot(first, w1_ref[0, 0:half, :], preferred_element_type=F32)
    q = jnp.dot(second, w1_ref[0, half:, :], preferred_element_type=F32)
    n = q.shape[0]
    h = p + pltpu.roll(q, n - 1, axis=0)
    h = h * jax.nn.sigmoid(h)
    o_ref[0, 0] = jnp.dot(h.astype(BF16), w2_ref[0], preferred_element_type=F32).astype(o_ref.dtype)


def compress(kt, pos, w1, w2):
    two, g, nch, half = kt.shape
    hid = w1.shape[2]
    dh = w2.shape[2]
    return pl.pallas_call(
        _compress_kernel,
        grid=(two, g),
        in_specs=[
            pl.BlockSpec((1, 1, nch, half), lambda a, b: (a, b, 0, 0)),
            pl.BlockSpec((1, 2, half), lambda a, b: (a, 0, 0)),
            pl.BlockSpec((1, 2 * half, hid), lambda a, b: (a, 0, 0)),
            pl.BlockSpec((1, hid, dh), lambda a, b: (a, 0, 0)),
        ],
        out_specs=pl.BlockSpec((1, 1, nch, dh), lambda a, b: (a, b, 0, 0)),
        out_shape=jax.ShapeDtypeStruct((two, g, nch, dh), BF16),
        compiler_params=_params("parallel", "parallel"),
        name="compress",
    )(kt, pos, w1, w2)


def _split3(x):
    hi = x.astype(BF16)
    r1 = x - hi.astype(F32)
    mid = r1.astype(BF16)
    lo = (r1 - mid.astype(F32)).astype(BF16)
    return hi, mid, lo


def _nsa_kernel(qt_ref, kc_ref, vct_ref, ka_ref, vat_ref, kw_ref, vwt_ref, glt_ref, o_ref,
                qa_ref, m_ref, acc_ref, s_ref, part_ref, *, seq):
    i = pl.program_id(1)
    s0 = i * Q_BLOCK
    dh = NSA_HEAD_DIM
    cols = NSA_GROUP * Q_BLOCK
    pair = 2 * Q_BLOCK
    n_sel = seq // SEL_LEN
    n_cmp = kc_ref.shape[1]

    qt = jnp.concatenate([qt_ref[r] for r in range(NSA_GROUP)], axis=1) * jnp.asarray(dh ** -0.5, BF16)
    t_col = s0 + (lax.broadcasted_iota(jnp.int32, (1, cols), 1) & (Q_BLOCK - 1))

    sc = jnp.dot(kc_ref[0], qt, preferred_element_type=F32)
    cmp_end = lax.broadcasted_iota(jnp.int32, (n_cmp, 1), 0) * CMP_STRIDE + (CMP_LEN - 1)
    mc = cmp_end <= t_col
    sc = jnp.where(mc, sc, NEG_INF)
    e = jnp.where(mc, jnp.exp(sc - jnp.max(sc, axis=0, keepdims=True)), 0.0)
    l = jnp.sum(e, axis=0, keepdims=True)
    pc = e * (1.0 / jnp.where(l > 0.0, l, 1.0))
    oc = jnp.dot(vct_ref[0], pc.astype(BF16), preferred_element_type=F32)

    wlen = WINDOW + Q_BLOCK
    w0 = pl.multiple_of(jnp.maximum(s0 - WINDOW, 0), Q_BLOCK)
    wc = jnp.maximum(i - WINDOW // Q_BLOCK, 0)
    sw = jnp.dot(kw_ref[0, pl.ds(w0, wlen), :], qt, preferred_element_type=F32)
    wpos = w0 + lax.broadcasted_iota(jnp.int32, (wlen, 1), 0)
    mw = (wpos <= t_col) & (wpos > t_col - WINDOW)
    sw = jnp.where(mw, sw, NEG_INF)
    pw = jnp.exp(sw - jnp.max(sw, axis=0, keepdims=True)).astype(BF16)
    accw = jnp.zeros((vwt_ref.shape[2], cols), F32)
    for j in range(wlen // Q_BLOCK):
        accw = accw + jnp.dot(vwt_ref[0, wc + j], pw[j * Q_BLOCK:(j + 1) * Q_BLOCK],
                              preferred_element_type=F32)
    ow = accw[0:dh] * (1.0 / accw[dh:dh + 1])

    def gate_row(b):
        gate = jax.nn.sigmoid(glt_ref[0].astype(F32))
        return jnp.concatenate(
            [gate[N_BRANCH * r + b:N_BRANCH * r + b + 1] for r in range(NSA_GROUP)], axis=1)

    part_ref[...] = gate_row(0) * oc + gate_row(2) * ow

    imp_c = pc[:, 0:Q_BLOCK]
    for r in range(1, NSA_GROUP):
        imp_c = imp_c + pc[:, r * Q_BLOCK:(r + 1) * Q_BLOCK]
    sj = lax.broadcasted_iota(jnp.int32, (n_sel, n_cmp), 0) * SEL_LEN
    cn = lax.broadcasted_iota(jnp.int32, (n_sel, n_cmp), 1) * CMP_STRIDE
    overlap = jnp.where((cn <= sj + SEL_LEN - 1) & (cn + CMP_LEN - 1 >= sj), 1.0, 0.0).astype(BF16)
    imp = jnp.zeros((n_sel, Q_BLOCK), F32)
    for part in _split3(imp_c):
        imp = imp + jnp.dot(overlap, part, preferred_element_type=F32)

    jj = lax.broadcasted_iota(jnp.int32, (n_sel, Q_BLOCK), 0)
    cur = (s0 + lax.broadcasted_iota(jnp.int32, (1, Q_BLOCK), 1)) >> (SEL_LEN.bit_length() - 1)
    forced = (jj == 0) | (jj == cur) | (jj == cur - 1)
    future = jj > cur
    work = jnp.where(future, NEG_INF, imp + jnp.where(forced, FORCE_BONUS, 0.0))
    jf = jj.astype(F32)
    chosen = jnp.zeros((n_sel, Q_BLOCK), F32)
    for _ in range(min(SEL_TOPK, n_sel)):
        mx = jnp.max(work, axis=0, keepdims=True)
        first = jnp.min(jnp.where(work == mx, jf, float(n_sel)), axis=0, keepdims=True)
        hit = jf == first
        chosen = jnp.where(hit, 1.0, chosen)
        work = jnp.where(hit, -jnp.inf, work)
    bias = jnp.where((chosen > 0.0) & jnp.logical_not(future), 0.0, NEG_INF).astype(BF16)
    qa_ref[0:n_sel, :] = jnp.concatenate([bias] * NSA_GROUP, axis=1)
    qa_ref[n_sel:, :] = qt

    m_ref[...] = jnp.full(m_ref.shape, NEG_INF, F32)
    acc_ref[...] = jnp.zeros(acc_ref.shape, F32)

    def scores(c, dst_ref):
        k0 = pl.multiple_of(c * SEL_CHUNK, SEL_CHUNK)
        ka = ka_ref[0, pl.ds(k0, SEL_CHUNK), :]
        for h0 in range(0, cols, pair):
            dst_ref[:, h0:h0 + pair] = jnp.dot(ka, qa_ref[:, h0:h0 + pair], preferred_element_type=F32)

    def accumulate(c, src_ref, causal):
        s = src_ref[...]
        if causal:
            kpos = c * SEL_CHUNK + lax.broadcasted_iota(jnp.int32, (SEL_CHUNK, 1), 0)
            s = jnp.where(kpos <= t_col, s, NEG_INF)
        m_old = m_ref[...]
        m_new = jnp.maximum(m_old, jnp.max(s, axis=0, keepdims=True))
        p = jnp.exp(s - m_new).astype(BF16)
        pv = jnp.dot(vat_ref[0, c], p, preferred_element_type=F32)
        acc_ref[...] = acc_ref[...] * jnp.exp(m_old - m_new) + pv
        m_ref[...] = m_new

    n_full = s0 >> (SEL_CHUNK.bit_length() - 1)
    sa_ref, sb_ref = s_ref.at[0], s_ref.at[1]
    scores(0, sa_ref)

    def body(j, carry):
        c = 2 * j
        scores(c + 1, sb_ref)
        accumulate(c, sa_ref, False)
        scores(c + 2, sa_ref)
        accumulate(c + 1, sb_ref, False)
        return carry

    lax.fori_loop(0, n_full >> 1, body, 0)

    @pl.when((n_full & 1) == 1)
    def _():
        scores(n_full, sb_ref)
        accumulate(n_full - 1, sa_ref, False)

    accumulate(n_full, s_ref.at[n_full & 1], True)
    acc = acc_ref[...]
    os_ = acc[0:dh] * (1.0 / acc[dh:dh + 1])

    out_t = part_ref[...] + gate_row(1) * os_
    out_t = jnp.concatenate([out_t[:, r * Q_BLOCK:(r + 1) * Q_BLOCK] for r in range(NSA_GROUP)], axis=0)
    o_ref[...] = out_t.T.astype(o_ref.dtype)


def nsa_attention(qt, kc, vct, k_aug, vat, kw, vwt, glt):
    h, dh, s = qt.shape
    g = kc.shape[0]
    dv = vat.shape[2]
    cols = NSA_GROUP * Q_BLOCK
    whole = lambda a: pl.BlockSpec((1,) + a.shape[1:], lambda gi, i: (gi,) + (0,) * (a.ndim - 1))
    return pl.pallas_call(
        functools.partial(_nsa_kernel, seq=s),
        grid=(g, s // Q_BLOCK),
        in_specs=[
            pl.BlockSpec((NSA_GROUP, dh, Q_BLOCK), lambda gi, i: (gi, 0, i)),
            whole(kc), whole(vct), whole(k_aug), whole(vat), whole(kw), whole(vwt),
            pl.BlockSpec((1, NSA_GROUP * N_BRANCH, Q_BLOCK), lambda gi, i: (gi, 0, i)),
        ],
        out_specs=pl.BlockSpec((Q_BLOCK, NSA_GROUP * dh), lambda gi, i: (i, gi)),
        out_shape=jax.ShapeDtypeStruct((s, h * dh), BF16),
        scratch_shapes=[pltpu.VMEM((k_aug.shape[2], cols), BF16),
                        pltpu.VMEM((1, cols), F32),
                        pltpu.VMEM((dv, cols), F32),
                        pltpu.VMEM((2, SEL_CHUNK, cols), F32),
                        pltpu.VMEM((dh, cols), F32)],
        compiler_params=_params("parallel", "arbitrary"),
        name="nsa_attention",
    )(qt, kc, vct, k_aug, vat, kw, vwt, glt)


def _out_proj_kernel(c_ref, o_ref, w_ref, x_ref, y_ref):
    half = c_ref.shape[1]
    acc = jnp.dot(c_ref[...], w_ref[0:half, :], preferred_element_type=F32)
    acc = acc + jnp.dot(o_ref[...], w_ref[half:, :], preferred_element_type=F32)
    y_ref[...] = x_ref[...] + acc


def out_proj(c, o, w, x, *, tm=1024, tn=512):
    m, half = c.shape
    n = w.shape[1]
    return pl.pallas_call(
        _out_proj_kernel,
        grid=(m // tm, n // tn),
        in_specs=[
            pl.BlockSpec((tm, half), lambda i, j: (i, 0)),
            pl.BlockSpec((tm, half), lambda i, j: (i, 0)),
            pl.BlockSpec((2 * half, tn), lambda i, j: (0, j)),
            pl.BlockSpec((tm, tn), lambda i, j: (i, j)),
        ],
        out_specs=pl.BlockSpec((tm, tn), lambda i, j: (i, j)),
        out_shape=jax.ShapeDtypeStruct((m, n), F32),
        compiler_params=_params("parallel", "parallel"),
        name="out_proj",
    )(c, o, w, x)


def _pool_kernel(x_ref, halo_ref, g_ref, w_ref, sc_ref, o_ref, *, tm):
    i = pl.program_id(0)
    x = x_ref[...]
    xn = _rms(x, g_ref[...])
    hn = _rms(halo_ref[...], g_ref[...]) * jnp.where(i > 0, 1.0, 0.0)
    ext = jnp.concatenate([hn, xn], axis=0)
    pd = w_ref.shape[1]
    pos1 = i * tm + 1 + lax.broadcasted_iota(jnp.int32, (tm, 1), 0)
    for gi, win in enumerate(POOL_WINDOWS):
        sl = slice(gi * pd, (gi + 1) * pd)
        s = ext[:, sl]
        sh = 1
        while sh < win:
            s = s + pltpu.roll(s, sh, axis=0)
            sh *= 2
        cnt = jnp.minimum(pos1, win).astype(F32)
        p = s[POOL_HALO:] / cnt - xn[:, sl]
        y = jnp.dot(p.astype(BF16), w_ref[gi], preferred_element_type=F32)
        o_ref[:, sl] = x[:, sl] + y * sc_ref[:, sl]


def pool_mixer(x, g, pool_w, pool_scale, *, tm=512):
    m, d = x.shape
    ng, pd, _ = pool_w.shape
    hb = tm // POOL_HALO
    return pl.pallas_call(
        functools.partial(_pool_kernel, tm=tm),
        grid=(m // tm,),
        in_specs=[
            pl.BlockSpec((tm, d), lambda i: (i, 0)),
            pl.BlockSpec((POOL_HALO, d), lambda i: (jnp.maximum(i * hb - 1, 0), 0)),
            pl.BlockSpec((1, d), lambda i: (0, 0)),
            pl.BlockSpec((ng, pd, pd), lambda i: (0, 0, 0)),
            pl.BlockSpec((1, d), lambda i: (0, 0)),
        ],
        out_specs=pl.BlockSpec((tm, d), lambda i: (i, 0)),
        out_shape=jax.ShapeDtypeStruct((m, d), F32),
        compiler_params=_params("parallel"),
        name="pool_mixer",
    )(x, x, g.reshape(1, d), pool_w, pool_scale.reshape(1, d))


def _mlp(x, g, w1, w2, final_g=None):
    return fused_mlp(x, g, w1, w2, final_g)


def _even_mixer(x, g, w_in, w_out, conv_w, conv_b, ln_g, ln_b, pos_k, pos_v, kw1, kw2, vw1, vw2):
    s, d = x.shape
    dh, gk, hq = NSA_HEAD_DIM, NSA_KV_HEADS, NSA_HEADS
    conv_cols = 2 * conv_w.shape[1]
    q_cols, kv_cols = hq * dh, gk * dh
    w_in = w_in.astype(BF16)
    w_nsa = w_in[:, conv_cols:]
    pad = (-w_nsa.shape[1]) % 256
    w_nsa = jnp.pad(w_nsa, ((0, 0), (0, pad)))

    proj_conv = norm_matmul(x, g, w_in[:, :conv_cols], out_dtype=F32)
    proj_nsa = norm_matmul(x, g, w_nsa, out_dtype=BF16, tn=256)
    c = conv_module(proj_conv, conv_w, conv_b, ln_g, ln_b)

    cut = lambda k: proj_nsa[:, q_cols + k * kv_cols:q_cols + (k + 1) * kv_cols]
    hm = lambda t: t.reshape(s, gk, dh).transpose(1, 0, 2)
    qt = proj_nsa[:, :q_cols].reshape(s, hq, dh).transpose(1, 2, 0)
    k_c, v_c, k_s, v_s, k_w, v_w = (hm(cut(k)) for k in range(6))
    glt = proj_nsa[:, q_cols + 6 * kv_cols:q_cols + 6 * kv_cols + hq * N_BRANCH]
    glt = glt.reshape(s, gk, NSA_GROUP * N_BRANCH).transpose(1, 2, 0)

    nch = s // CMP_STRIDE
    kt = jnp.stack([k_c, v_c]).reshape(2, gk, nch, CMP_STRIDE * dh)
    pos = jnp.stack([pos_k, pos_v]).reshape(2, 2, CMP_STRIDE * dh)
    kvc = compress(kt, pos, jnp.stack([kw1, vw1]).astype(BF16), jnp.stack([kw2, vw2]).astype(BF16))

    n_sel = s // SEL_LEN
    onehot = (jnp.arange(s)[:, None] // SEL_LEN == jnp.arange(n_sel)[None, :]).astype(BF16)
    k_aug = jnp.concatenate([jnp.broadcast_to(onehot, (gk, s, n_sel)), k_s], axis=-1)
    ones = jnp.ones((gk, s, ONES_ROWS), BF16)
    chunk_t = lambda v, n: (jnp.concatenate([v, ones], axis=-1)
                            .reshape(gk, s // n, n, dh + ONES_ROWS).transpose(0, 1, 3, 2))
    o = nsa_attention(qt, kvc[0], kvc[1].transpose(0, 2, 1), k_aug, chunk_t(v_s, SEL_CHUNK),
                      k_w, chunk_t(v_w, Q_BLOCK), glt)
    return out_proj(c, o, w_out.astype(BF16), x)


def kernel(x, mix_norm, mlp_norm, w_mlp_in, w_mlp_out, w_in, w_out, conv_w, conv_b, conv_ln_g,
           conv_ln_b, cmp_pos_k, cmp_pos_v, cmp_k_w1, cmp_k_w2, cmp_v_w1, cmp_v_w2, pool_w,
           pool_scale, final_norm):
    b, s, d = x.shape
    depth = mix_norm.shape[0]
    outs = []
    for bi in range(b):
        h = x[bi]
        for layer in range(depth):
            i = layer // 2
            if layer % 2 == 0:
                h = _even_mixer(h, mix_norm[layer], w_in[i], w_out[i], conv_w[i], conv_b[i],
                                conv_ln_g[i], conv_ln_b[i], cmp_pos_k[i], cmp_pos_v[i],
                                cmp_k_w1[i], cmp_k_w2[i], cmp_v_w1[i], cmp_v_w2[i])
            else:
                h = pool_mixer(h, mix_norm[layer], pool_w[i].astype(BF16), pool_scale[i])
            fg = final_norm if layer == depth - 1 else None
            h = _mlp(h, mlp_norm[layer], w_mlp_in[layer], w_mlp_out[layer], fg)
        outs.append(h)
    return jnp.stack(outs)
```

```python
import functools

import jax
import jax.numpy as jnp
from jax import lax
from jax.experimental import pallas as pl
from jax.experimental.pallas import tpu as pltpu

F32 = jnp.float32
BF16 = jnp.bfloat16

EPS = 1e-6
NEG_INF = -1e30
CONV_WIDTH = 31
SUBLANES = 8
CONV_HALO = 32
NSA_HEADS = 16
NSA_HEAD_DIM = 64
NSA_KV_HEADS = 4
NSA_GROUP = NSA_HEADS // NSA_KV_HEADS
N_BRANCH = 3
CMP_LEN = 32
CMP_STRIDE = 16
SEL_LEN = 64
SEL_TOPK = 16
WINDOW = 512
Q_BLOCK = 128
FORCE_BONUS = 1e3
POOL_WINDOWS = (2, 4, 8, 16)
POOL_HALO = 16
SEL_CHUNK = 512
ONES_ROWS = 16

VMEM_LIMIT = 56 * 1024 * 1024


def _params(*sem):
    return pltpu.CompilerParams(dimension_semantics=sem, vmem_limit_bytes=VMEM_LIMIT)


def _rms(x, g):
    r = lax.rsqrt(jnp.mean(x * x, axis=-1, keepdims=True) + EPS)
    return x * r * g


def _norm_mm_kernel(x_ref, g_ref, w_ref, o_ref, xn_ref, *, act):
    @pl.when(pl.program_id(1) == 0)
    def _():
        xn_ref[...] = _rms(x_ref[...], g_ref[...]).astype(BF16)

    acc = jnp.dot(xn_ref[...], w_ref[...], preferred_element_type=F32)
    if act:
        acc = jnp.square(jnp.maximum(acc, 0.0))
    o_ref[...] = acc.astype(o_ref.dtype)


def norm_matmul(x, g, w, *, out_dtype, act=False, tm=1024, tn=512):
    m, k = x.shape
    n = w.shape[1]
    assert m % tm == 0 and n % tn == 0
    return pl.pallas_call(
        functools.partial(_norm_mm_kernel, act=act),
        grid=(m // tm, n // tn),
        in_specs=[
            pl.BlockSpec((tm, k), lambda i, j: (i, 0)),
            pl.BlockSpec((1, k), lambda i, j: (0, 0)),
            pl.BlockSpec((k, tn), lambda i, j: (0, j)),
        ],
        out_specs=pl.BlockSpec((tm, tn), lambda i, j: (i, j)),
        out_shape=jax.ShapeDtypeStruct((m, n), out_dtype),
        scratch_shapes=[pltpu.VMEM((tm, k), BF16)],
        compiler_params=_params("parallel", "arbitrary"),
        name="norm_matmul",
    )(x, g.reshape(1, k), w)


def _mlp_kernel(x_ref, g_ref, w1_ref, w2_ref, gf_ref, o_ref, xn_ref, h_ref, *, final_norm):
    f = pl.program_id(1)

    @pl.when(f == 0)
    def _():
        x = x_ref[...]
        xn_ref[...] = _rms(x, g_ref[...]).astype(BF16)
        o_ref[...] = x
        h_ref[...] = jnp.zeros(h_ref.shape, BF16)

    h_new = jnp.dot(xn_ref[...], w1_ref[...].astype(BF16), preferred_element_type=F32)
    o_ref[...] += jnp.dot(h_ref[...], w2_ref[...].astype(BF16), preferred_element_type=F32)
    h_ref[...] = jnp.square(jnp.maximum(h_new, 0.0)).astype(BF16)

    if final_norm:
        @pl.when(f == pl.num_programs(1) - 1)
        def _():
            o_ref[...] = _rms(o_ref[...], gf_ref[...])


def fused_mlp(x, g, w1, w2, layer, gf=None, *, tm=2048, tf=256):
    m, d = x.shape
    nf = w1.shape[2] // tf
    final_norm = gf is not None
    if gf is None:
        gf = g
    once = pl.Buffered(1)
    return pl.pallas_call(
        functools.partial(_mlp_kernel, final_norm=final_norm),
        grid=(m // tm, nf + 1),
        in_specs=[
            pl.BlockSpec((tm, d), lambda i, f: (i, 0), pipeline_mode=once),
            pl.BlockSpec((1, d), lambda i, f: (0, 0)),
            pl.BlockSpec((None, d, tf), lambda i, f: (layer, 0, jnp.minimum(f, nf - 1))),
            pl.BlockSpec((None, tf, d), lambda i, f: (layer, jnp.maximum(f - 1, 0), 0)),
            pl.BlockSpec((1, d), lambda i, f: (0, 0)),
        ],
        out_specs=pl.BlockSpec((tm, d), lambda i, f: (i, 0), pipeline_mode=once),
        out_shape=jax.ShapeDtypeStruct((m, d), F32),
        scratch_shapes=[pltpu.VMEM((tm, d), BF16), pltpu.VMEM((tm, tf), BF16)],
        compiler_params=_params("parallel", "arbitrary"),
        name="fused_mlp",
    )(x, g.reshape(1, d), w1, w2, gf.reshape(1, d))


def _conv_kernel(a_ref, gt_ref, w_ref, b_ref, lg_ref, lb_ref, o_ref, cs_ref, y_ref, wb_ref, *, tt):
    i = pl.program_id(0)

    @pl.when(i == 0)
    def _():
        cs_ref[0, 0:CONV_HALO, :] = jnp.zeros((CONV_HALO, cs_ref.shape[2]), F32)
        for k in range(CONV_WIDTH):
            wb_ref[k] = jnp.broadcast_to(w_ref[k:k + 1, :], wb_ref.shape[1:])

    @pl.when(i > 0)
    def _():
        cs_ref[0, 0:CONV_HALO, :] = cs_ref[0, tt:tt + CONV_HALO, :]

    cs_ref[0, CONV_HALO:, :] = a_ref[...] * jax.nn.sigmoid(gt_ref[...])
    c_ext = cs_ref[0]
    for b in range(1, SUBLANES):
        cs_ref[b] = pltpu.roll(c_ext, b, axis=0)

    rows = 4 * SUBLANES

    def chunk(j, carry):
        r0 = pl.multiple_of(j * rows, rows)
        groups = range(rows // SUBLANES)
        acc = [jnp.zeros((SUBLANES, cs_ref.shape[2]), F32) + b_ref[...] for _ in groups]
        for k in range(CONV_WIDTH):
            a, b = divmod(k, SUBLANES)
            wk = wb_ref[CONV_WIDTH - 1 - k]
            for q in groups:
                lo = pl.multiple_of(r0 + CONV_HALO + (q - a) * SUBLANES, SUBLANES)
                acc[q] = acc[q] + cs_ref[b, pl.ds(lo, SUBLANES), :] * wk
        for q in groups:
            y_ref[pl.ds(pl.multiple_of(r0 + q * SUBLANES, SUBLANES), SUBLANES), :] = acc[q]
        return carry

    lax.fori_loop(0, tt // rows, chunk, 0)

    acc = y_ref[...]
    mu = jnp.mean(acc, axis=-1, keepdims=True)
    xc = acc - mu
    var = jnp.mean(xc * xc, axis=-1, keepdims=True)
    y = xc * lax.rsqrt(var + EPS) * lg_ref[...] + lb_ref[...]
    o_ref[...] = (y * jax.nn.sigmoid(y)).astype(o_ref.dtype)


def conv_module(proj, conv_w, conv_b, ln_g, ln_b, *, tt=512):
    s = proj.shape[0]
    c = conv_w.shape[1]
    row = lambda v: v.reshape(1, c)
    return pl.pallas_call(
        functools.partial(_conv_kernel, tt=tt),
        grid=(s // tt,),
        in_specs=[
            pl.BlockSpec((tt, c), lambda i: (i, 0)),
            pl.BlockSpec((tt, c), lambda i: (i, 1)),
            pl.BlockSpec((CONV_WIDTH, c), lambda i: (0, 0)),
            pl.BlockSpec((1, c), lambda i: (0, 0)),
            pl.BlockSpec((1, c), lambda i: (0, 0)),
            pl.BlockSpec((1, c), lambda i: (0, 0)),
        ],
        out_specs=pl.BlockSpec((tt, c), lambda i: (i, 0)),
        out_shape=jax.ShapeDtypeStruct((s, c), BF16),
        scratch_shapes=[pltpu.VMEM((SUBLANES, tt + CONV_HALO, c), F32), pltpu.VMEM((tt, c), F32),
                        pltpu.VMEM((CONV_WIDTH, SUBLANES, c), F32)],
        compiler_params=_params("arbitrary"),
        name="conv_module",
    )(proj, proj, conv_w, row(conv_b), row(ln_g), row(ln_b))


def _compress_kernel(kt_ref, pos_ref, w1_ref, w2_ref, o_ref):
    kt = kt_ref[0, 0].astype(F32)
    half = kt.shape[1]
    first = (kt + pos_ref[0, 0:1, :]).astype(BF16)
    second = (kt + pos_ref[0, 1:2, :]).astype(BF16)
    p = jnp.dot(first, w1_ref[0, 0:half, :], preferred_element_type=F32)
    q = jnp.dot(second, w1_ref[0, half:, :], preferred_element_type=F32)
    n = q.shape[0]
    h = p + pltpu.roll(q, n - 1, axis=0)
    h = h * jax.nn.sigmoid(h)
    o_ref[0, 0] = jnp.dot(h.astype(BF16), w2_ref[0], preferred_element_type=F32).astype(o_ref.dtype)


def compress(kt, pos, w1, w2):
    two, g, nch, half = kt.shape
    hid = w1.shape[2]
    dh = w2.shape[2]
    return pl.pallas_call(
        _compress_kernel,
        grid=(two, g),
        in_specs=[
            pl.BlockSpec((1, 1, nch, half), lambda a, b: (a, b, 0, 0)),
            pl.BlockSpec((1, 2, half), lambda a, b: (a, 0, 0)),
            pl.BlockSpec((1, 2 * half, hid), lambda a, b: (a, 0, 0)),
            pl.BlockSpec((1, hid, dh), lambda a, b: (a, 0, 0)),
        ],
        out_specs=pl.BlockSpec((1, 1, nch, dh), lambda a, b: (a, b, 0, 0)),
        out_shape=jax.ShapeDtypeStruct((two, g, nch, dh), BF16),
        compiler_params=_params("parallel", "parallel"),
        name="compress",
    )(kt, pos, w1, w2)


def _split3(x):
    hi = x.astype(BF16)
    r1 = x - hi.astype(F32)
    mid = r1.astype(BF16)
    lo = (r1 - mid.astype(F32)).astype(BF16)
    return hi, mid, lo


def _nsa_kernel(qt_ref, kc_ref, vct_ref, ka_ref, vat_ref, kw_ref, vwt_ref, glt_ref, o_ref,
                qa_ref, m_ref, acc_ref, s_ref, part_ref, *, seq):
    i = pl.program_id(1)
    s0 = i * Q_BLOCK
    dh = NSA_HEAD_DIM
    cols = NSA_GROUP * Q_BLOCK
    pair = 2 * Q_BLOCK
    n_sel = seq // SEL_LEN
    n_cmp = kc_ref.shape[1]

    qt = jnp.concatenate([qt_ref[r] for r in range(NSA_GROUP)], axis=1) * jnp.asarray(dh ** -0.5, BF16)
    t_col = s0 + (lax.broadcasted_iota(jnp.int32, (1, cols), 1) & (Q_BLOCK - 1))

    sc = jnp.dot(kc_ref[0], qt, preferred_element_type=F32)
    cmp_end = lax.broadcasted_iota(jnp.int32, (n_cmp, 1), 0) * CMP_STRIDE + (CMP_LEN - 1)
    mc = cmp_end <= t_col
    sc = jnp.where(mc, sc, NEG_INF)
    e = jnp.where(mc, jnp.exp(sc - jnp.max(sc, axis=0, keepdims=True)), 0.0)
    l = jnp.sum(e, axis=0, keepdims=True)
    pc = e * (1.0 / jnp.where(l > 0.0, l, 1.0))
    oc = jnp.dot(vct_ref[0], pc.astype(BF16), preferred_element_type=F32)

    wlen = WINDOW + Q_BLOCK
    w0 = pl.multiple_of(jnp.maximum(s0 - WINDOW, 0), Q_BLOCK)
    wc = jnp.maximum(i - WINDOW // Q_BLOCK, 0)
    sw = jnp.dot(kw_ref[0, pl.ds(w0, wlen), :], qt, preferred_element_type=F32)
    wpos = w0 + lax.broadcasted_iota(jnp.int32, (wlen, 1), 0)
    mw = (wpos <= t_col) & (wpos > t_col - WINDOW)
    sw = jnp.where(mw, sw, NEG_INF)
    pw = jnp.exp(sw - jnp.max(sw, axis=0, keepdims=True)).astype(BF16)
    accw = jnp.zeros((vwt_ref.shape[2], cols), F32)
    for j in range(wlen // Q_BLOCK):
        accw = accw + jnp.dot(vwt_ref[0, wc + j], pw[j * Q_BLOCK:(j + 1) * Q_BLOCK],
                              preferred_element_type=F32)
    ow = accw[0:dh] * (1.0 / accw[dh:dh + 1])

    def gate_row(b):
        gate = jax.nn.sigmoid(glt_ref[0].astype(F32))
        return jnp.concatenate(
            [gate[N_BRANCH * r + b:N_BRANCH * r + b + 1] for r in range(NSA_GROUP)], axis=1)

    part_ref[...] = gate_row(0) * oc + gate_row(2) * ow

    imp_c = pc[:, 0:Q_BLOCK]
    for r in range(1, NSA_GROUP):
        imp_c = imp_c + pc[:, r * Q_BLOCK:(r + 1) * Q_BLOCK]
    sj = lax.broadcasted_iota(jnp.int32, (n_sel, n_cmp), 0) * SEL_LEN
    cn = lax.broadcasted_iota(jnp.int32, (n_sel, n_cmp), 1) * CMP_STRIDE
    overlap = jnp.where((cn <= sj + SEL_LEN - 1) & (cn + CMP_LEN - 1 >= sj), 1.0, 0.0).astype(BF16)
    imp = jnp.zeros((n_sel, Q_BLOCK), F32)
    for part in _split3(imp_c):
        imp = imp + jnp.dot(overlap, part, preferred_element_type=F32)

    jj = lax.broadcasted_iota(jnp.int32, (n_sel, Q_BLOCK), 0)
    cur = (s0 + lax.broadcasted_iota(jnp.int32, (1, Q_BLOCK), 1)) >> (SEL_LEN.bit_length() - 1)
    forced = (jj == 0) | (jj == cur) | (jj == cur - 1)
    future = jj > cur
    work = jnp.where(future, NEG_INF, imp + jnp.where(forced, FORCE_BONUS, 0.0))
    jf = jj.astype(F32)
    chosen = jnp.zeros((n_sel, Q_BLOCK), F32)
    for _ in range(min(SEL_TOPK, n_sel)):
        mx = jnp.max(work, axis=0, keepdims=True)
        first = jnp.min(jnp.where(work == mx, jf, float(n_sel)), axis=0, keepdims=True)
        hit = jf == first
        chosen = jnp.where(hit, 1.0, chosen)
        work = jnp.where(hit, -jnp.inf, work)
    bias = jnp.where((chosen > 0.0) & jnp.logical_not(future), 0.0, NEG_INF).astype(BF16)
    qa_ref[0:n_sel, :] = jnp.concatenate([bias] * NSA_GROUP, axis=1)
    qa_ref[n_sel:, :] = qt

    m_ref[...] = jnp.full(m_ref.shape, NEG_INF, F32)
    acc_ref[...] = jnp.zeros(acc_ref.shape, F32)

    def scores(c, dst_ref):
        k0 = pl.multiple_of(c * SEL_CHUNK, SEL_CHUNK)
        ka = ka_ref[0, pl.ds(k0, SEL_CHUNK), :]
        for h0 in range(0, cols, pair):
            dst_ref[:, h0:h0 + pair] = jnp.dot(ka, qa_ref[:, h0:h0 + pair], preferred_element_type=F32)

    def accumulate(c, src_ref, causal):
        s = src_ref[...]
        if causal:
            kpos = c * SEL_CHUNK + lax.broadcasted_iota(jnp.int32, (SEL_CHUNK, 1), 0)
            s = jnp.where(kpos <= t_col, s, NEG_INF)
        m_old = m_ref[...]
        m_new = jnp.maximum(m_old, jnp.max(s, axis=0, keepdims=True))
        p = jnp.exp(s - m_new).astype(BF16)
        pv = jnp.dot(vat_ref[0, c], p, preferred_element_type=F32)
        acc_ref[...] = acc_ref[...] * jnp.exp(m_old - m_new) + pv
        m_ref[...] = m_new

    n_full = s0 >> (SEL_CHUNK.bit_length() - 1)
    sa_ref, sb_ref = s_ref.at[0], s_ref.at[1]
    scores(0, sa_ref)

    def body(j, carry):
        c = 2 * j
        scores(c + 1, sb_ref)
        accumulate(c, sa_ref, False)
        scores(c + 2, sa_ref)
        accumulate(c + 1, sb_ref, False)
        return carry

    lax.fori_loop(0, n_full >> 1, body, 0)

    @pl.when((n_full & 1) == 1)
    def _():
        scores(n_full, sb_ref)
        accumulate(n_full - 1, sa_ref, False)

    accumulate(n_full, s_ref.at[n_full & 1], True)
    acc = acc_ref[...]
    os_ = acc[0:dh] * (1.0 / acc[dh:dh + 1])

    out_t = part_ref[...] + gate_row(1) * os_
    out_t = jnp.concatenate([out_t[:, r * Q_BLOCK:(r + 1) * Q_BLOCK] for r in range(NSA_GROUP)], axis=0)
    o_ref[...] = out_t.T.astype(o_ref.dtype)


def nsa_attention(qt, kc, vct, k_aug, vat, kw, vwt, glt):
    h, dh, s = qt.shape
    g = kc.shape[0]
    dv = vat.shape[2]
    cols = NSA_GROUP * Q_BLOCK
    whole = lambda a: pl.BlockSpec((1,) + a.shape[1:], lambda gi, i: (gi,) + (0,) * (a.ndim - 1))
    return pl.pallas_call(
        functools.partial(_nsa_kernel, seq=s),
        grid=(g, s // Q_BLOCK),
        in_specs=[
            pl.BlockSpec((NSA_GROUP, dh, Q_BLOCK), lambda gi, i: (gi, 0, i)),
            whole(kc), whole(vct), whole(k_aug), whole(vat), whole(kw), whole(vwt),
            pl.BlockSpec((1, NSA_GROUP * N_BRANCH, Q_BLOCK), lambda gi, i: (gi, 0, i)),
        ],
        out_specs=pl.BlockSpec((Q_BLOCK, NSA_GROUP * dh), lambda gi, i: (i, gi)),
        out_shape=jax.ShapeDtypeStruct((s, h * dh), BF16),
        scratch_shapes=[pltpu.VMEM((k_aug.shape[2], cols), BF16),
                        pltpu.VMEM((1, cols), F32),
                        pltpu.VMEM((dv, cols), F32),
                        pltpu.VMEM((2, SEL_CHUNK, cols), F32),
                        pltpu.VMEM((dh, cols), F32)],
        compiler_params=_params("parallel", "arbitrary"),
        name="nsa_attention",
    )(qt, kc, vct, k_aug, vat, kw, vwt, glt)


def _out_proj_kernel(c_ref, o_ref, w_ref, x_ref, y_ref):
    half = c_ref.shape[1]
    acc = jnp.dot(c_ref[...], w_ref[0:half, :], preferred_element_type=F32)
    acc = acc + jnp.dot(o_ref[...], w_ref[half:, :], preferred_element_type=F32)
    y_ref[...] = x_ref[...] + acc


def out_proj(c, o, w, x, *, tm=1024, tn=512):
    m, half = c.shape
    n = w.shape[1]
    return pl.pallas_call(
        _out_proj_kernel,
        grid=(m // tm, n // tn),
        in_specs=[
            pl.BlockSpec((tm, half), lambda i, j: (i, 0)),
            pl.BlockSpec((tm, half), lambda i, j: (i, 0)),
            pl.BlockSpec((2 * half, tn), lambda i, j: (0, j)),
            pl.BlockSpec((tm, tn), lambda i, j: (i, j)),
        ],
        out_specs=pl.BlockSpec((tm, tn), lambda i, j: (i, j)),
        out_shape=jax.ShapeDtypeStruct((m, n), F32),
        compiler_params=_params("parallel", "parallel"),
        name="out_proj",
    )(c, o, w, x)


def _pool_kernel(x_ref, halo_ref, g_ref, w_ref, sc_ref, o_ref, *, tm):
    i = pl.program_id(0)
    x = x_ref[...]
    xn = _rms(x, g_ref[...])
    hn = _rms(halo_ref[...], g_ref[...]) * jnp.where(i > 0, 1.0, 0.0)
    ext = jnp.concatenate([hn, xn], axis=0)
    pd = w_ref.shape[1]
    pos1 = i * tm + 1 + lax.broadcasted_iota(jnp.int32, (tm, 1), 0)
    for gi, win in enumerate(POOL_WINDOWS):
        sl = slice(gi * pd, (gi + 1) * pd)
        s = ext[:, sl]
        sh = 1
        while sh < win:
            s = s + pltpu.roll(s, sh, axis=0)
            sh *= 2
        cnt = jnp.minimum(pos1, win).astype(F32)
        p = s[POOL_HALO:] / cnt - xn[:, sl]
        y = jnp.dot(p.astype(BF16), w_ref[gi], preferred_element_type=F32)
        o_ref[:, sl] = x[:, sl] + y * sc_ref[:, sl]


def pool_mixer(x, g, pool_w, pool_scale, *, tm=512):
    m, d = x.shape
    ng, pd, _ = pool_w.shape
    hb = tm // POOL_HALO
    return pl.pallas_call(
        functools.partial(_pool_kernel, tm=tm),
        grid=(m // tm,),
        in_specs=[
            pl.BlockSpec((tm, d), lambda i: (i, 0)),
            pl.BlockSpec((POOL_HALO, d), lambda i: (jnp.maximum(i * hb - 1, 0), 0)),
            pl.BlockSpec((1, d), lambda i: (0, 0)),
            pl.BlockSpec((ng, pd, pd), lambda i: (0, 0, 0)),
            pl.BlockSpec((1, d), lambda i: (0, 0)),
        ],
        out_specs=pl.BlockSpec((tm, d), lambda i: (i, 0)),
        out_shape=jax.ShapeDtypeStruct((m, d), F32),
        compiler_params=_params("parallel"),
        name="pool_mixer",
    )(x, x, g.reshape(1, d), pool_w, pool_scale.reshape(1, d))


def _even_mixer(x, g, w_in, w_out, conv_w, conv_b, ln_g, ln_b, pos_k, pos_v, kw1, kw2, vw1, vw2):
    s, d = x.shape
    dh, gk, hq = NSA_HEAD_DIM, NSA_KV_HEADS, NSA_HEADS
    conv_cols = 2 * conv_w.shape[1]
    q_cols, kv_cols = hq * dh, gk * dh
    w_in = w_in.astype(BF16)
    w_nsa = w_in[:, conv_cols:]
    pad = (-w_nsa.shape[1]) % 256
    w_nsa = jnp.pad(w_nsa, ((0, 0), (0, pad)))

    proj_conv = norm_matmul(x, g, w_in[:, :conv_cols], out_dtype=F32)
    proj_nsa = norm_matmul(x, g, w_nsa, out_dtype=BF16, tn=256)
    c = conv_module(proj_conv, conv_w, conv_b, ln_g, ln_b)

    cut = lambda k: proj_nsa[:, q_cols + k * kv_cols:q_cols + (k + 1) * kv_cols]
    hm = lambda t: t.reshape(s, gk, dh).transpose(1, 0, 2)
    qt = proj_nsa[:, :q_cols].reshape(s, hq, dh).transpose(1, 2, 0)
    k_c, v_c, k_s, v_s, k_w, v_w = (hm(cut(k)) for k in range(6))
    glt = proj_nsa[:, q_cols + 6 * kv_cols:q_cols + 6 * kv_cols + hq * N_BRANCH]
    glt = glt.reshape(s, gk, NSA_GROUP * N_BRANCH).transpose(1, 2, 0)

    nch = s // CMP_STRIDE
    kt = jnp.stack([k_c, v_c]).reshape(2, gk, nch, CMP_STRIDE * dh)
    pos = jnp.stack([pos_k, pos_v]).reshape(2, 2, CMP_STRIDE * dh)
    kvc = compress(kt, pos, jnp.stack([kw1, vw1]).astype(BF16), jnp.stack([kw2, vw2]).astype(BF16))

    n_sel = s // SEL_LEN
    onehot = (jnp.arange(s)[:, None] // SEL_LEN == jnp.arange(n_sel)[None, :]).astype(BF16)
    k_aug = jnp.concatenate([jnp.broadcast_to(onehot, (gk, s, n_sel)), k_s], axis=-1)
    ones = jnp.ones((gk, s, ONES_ROWS), BF16)
    chunk_t = lambda v, n: (jnp.concatenate([v, ones], axis=-1)
                            .reshape(gk, s // n, n, dh + ONES_ROWS).transpose(0, 1, 3, 2))
    o = nsa_attention(qt, kvc[0], kvc[1].transpose(0, 2, 1), k_aug, chunk_t(v_s, SEL_CHUNK),
                      k_w, chunk_t(v_w, Q_BLOCK), glt)
    return out_proj(c, o, w_out.astype(BF16), x)


def kernel(x, mix_norm, mlp_norm, w_mlp_in, w_mlp_out, w_in, w_out, conv_w, conv_b, conv_ln_g,
           conv_ln_b, cmp_pos_k, cmp_pos_v, cmp_k_w1, cmp_k_w2, cmp_v_w1, cmp_v_w2, pool_w,
           pool_scale, final_norm):
    b, s, d = x.shape
    depth = mix_norm.shape[0]
    outs = []
    for bi in range(b):
        h = x[bi]
        for layer in range(depth):
            i = layer // 2
            if layer % 2 == 0:
                h = _even_mixer(h, mix_norm[layer], w_in[i], w_out[i], conv_w[i], conv_b[i],
                                conv_ln_g[i], conv_ln_b[i], cmp_pos_k[i], cmp_pos_v[i],
                                cmp_k_w1[i], cmp_k_w2[i], cmp_v_w1[i], cmp_v_w2[i])
            else:
                h = pool_mixer(h, mix_norm[layer], pool_w[i].astype(BF16), pool_scale[i])
            fg = final_norm if layer == depth - 1 else None
            h = fused_mlp(h, mlp_norm[layer], w_mlp_in, w_mlp_out, layer, fg)
        outs.append(h)
    return jnp.stack(outs)
```

```python
import functools

import jax
import jax.numpy as jnp
from jax import lax
from jax.experimental import pallas as pl
from jax.experimental.pallas import tpu as pltpu

F32 = jnp.float32
BF16 = jnp.bfloat16

EPS = 1e-6
NEG_INF = -1e30
LOG2E = 1.4426950408889634
CONV_WIDTH = 31
SUBLANES = 8
CONV_HALO = 32
NSA_HEADS = 16
NSA_HEAD_DIM = 64
NSA_KV_HEADS = 4
NSA_GROUP = NSA_HEADS // NSA_KV_HEADS
N_BRANCH = 3
CMP_LEN = 32
CMP_STRIDE = 16
SEL_LEN = 64
SEL_TOPK = 16
WINDOW = 512
Q_BLOCK = 256
FORCE_BONUS = 1e3
POOL_WINDOWS = (2, 4, 8, 16)
POOL_HALO = 16
SEL_CHUNK = 512
ONES_ROWS = 16

VMEM_LIMIT = 56 * 1024 * 1024


def _params(*sem):
    return pltpu.CompilerParams(dimension_semantics=sem, vmem_limit_bytes=VMEM_LIMIT)


def _rms(x, g):
    r = lax.rsqrt(jnp.mean(x * x, axis=-1, keepdims=True) + EPS)
    return x * r * g


def _norm_mm_kernel(x_ref, g_ref, w_ref, o_ref, xn_ref, *, act):
    @pl.when(pl.program_id(1) == 0)
    def _():
        xn_ref[...] = _rms(x_ref[...], g_ref[...]).astype(BF16)

    acc = jnp.dot(xn_ref[...], w_ref[...], preferred_element_type=F32)
    if act:
        acc = jnp.square(jnp.maximum(acc, 0.0))
    o_ref[...] = acc.astype(o_ref.dtype)


def norm_matmul(x, g, w, *, out_dtype, act=False, tm=1024, tn=512):
    m, k = x.shape
    n = w.shape[1]
    assert m % tm == 0 and n % tn == 0
    return pl.pallas_call(
        functools.partial(_norm_mm_kernel, act=act),
        grid=(m // tm, n // tn),
        in_specs=[
            pl.BlockSpec((tm, k), lambda i, j: (i, 0)),
            pl.BlockSpec((1, k), lambda i, j: (0, 0)),
            pl.BlockSpec((k, tn), lambda i, j: (0, j)),
        ],
        out_specs=pl.BlockSpec((tm, tn), lambda i, j: (i, j)),
        out_shape=jax.ShapeDtypeStruct((m, n), out_dtype),
        scratch_shapes=[pltpu.VMEM((tm, k), BF16)],
        compiler_params=_params("parallel", "arbitrary"),
        name="norm_matmul",
    )(x, g.reshape(1, k), w)


def _mlp_kernel(x_ref, g_ref, w1_ref, w2_ref, gf_ref, o_ref, xn_ref, h_ref, *, final_norm):
    f = pl.program_id(1)

    @pl.when(f == 0)
    def _():
        x = x_ref[...]
        xn_ref[...] = _rms(x, g_ref[...]).astype(BF16)
        o_ref[...] = x
        h_ref[...] = jnp.zeros(h_ref.shape, BF16)

    h_new = jnp.dot(xn_ref[...], w1_ref[...].astype(BF16), preferred_element_type=F32)
    o_ref[...] += jnp.dot(h_ref[...], w2_ref[...].astype(BF16), preferred_element_type=F32)
    h_ref[...] = jnp.square(jnp.maximum(h_new, 0.0)).astype(BF16)

    if final_norm:
        @pl.when(f == pl.num_programs(1) - 1)
        def _():
            o_ref[...] = _rms(o_ref[...], gf_ref[...])


def fused_mlp(x, g, w1, w2, layer, gf=None, *, tm=2048, tf=256):
    m, d = x.shape
    nf = w1.shape[2] // tf
    final_norm = gf is not None
    if gf is None:
        gf = g
    once = pl.Buffered(1)
    return pl.pallas_call(
        functools.partial(_mlp_kernel, final_norm=final_norm),
        grid=(m // tm, nf + 1),
        in_specs=[
            pl.BlockSpec((tm, d), lambda i, f: (i, 0), pipeline_mode=once),
            pl.BlockSpec((1, d), lambda i, f: (0, 0)),
            pl.BlockSpec((None, d, tf), lambda i, f: (layer, 0, jnp.minimum(f, nf - 1))),
            pl.BlockSpec((None, tf, d), lambda i, f: (layer, jnp.maximum(f - 1, 0), 0)),
            pl.BlockSpec((1, d), lambda i, f: (0, 0)),
        ],
        out_specs=pl.BlockSpec((tm, d), lambda i, f: (i, 0), pipeline_mode=once),
        out_shape=jax.ShapeDtypeStruct((m, d), F32),
        scratch_shapes=[pltpu.VMEM((tm, d), BF16), pltpu.VMEM((tm, tf), BF16)],
        compiler_params=_params("parallel", "arbitrary"),
        name="fused_mlp",
    )(x, g.reshape(1, d), w1, w2, gf.reshape(1, d))


def _conv_kernel(a_ref, gt_ref, w_ref, b_ref, lg_ref, lb_ref, o_ref, cs_ref, y_ref, wb_ref, *, tt):
    i = pl.program_id(0)

    @pl.when(i == 0)
    def _():
        cs_ref[0, 0:CONV_HALO, :] = jnp.zeros((CONV_HALO, cs_ref.shape[2]), F32)
        for k in range(CONV_WIDTH):
            wb_ref[k] = jnp.broadcast_to(w_ref[k:k + 1, :], wb_ref.shape[1:])

    @pl.when(i > 0)
    def _():
        cs_ref[0, 0:CONV_HALO, :] = cs_ref[0, tt:tt + CONV_HALO, :]

    cs_ref[0, CONV_HALO:, :] = a_ref[...].astype(F32) * jax.nn.sigmoid(gt_ref[...].astype(F32))
    c_ext = cs_ref[0]
    for b in range(1, SUBLANES):
        cs_ref[b] = pltpu.roll(c_ext, b, axis=0)

    rows = 4 * SUBLANES

    def chunk(j, carry):
        r0 = pl.multiple_of(j * rows, rows)
        groups = range(rows // SUBLANES)
        acc = [jnp.zeros((SUBLANES, cs_ref.shape[2]), F32) + b_ref[...] for _ in groups]
        for k in range(CONV_WIDTH):
            a, b = divmod(k, SUBLANES)
            wk = wb_ref[CONV_WIDTH - 1 - k]
            for q in groups:
                lo = pl.multiple_of(r0 + CONV_HALO + (q - a) * SUBLANES, SUBLANES)
                acc[q] = acc[q] + cs_ref[b, pl.ds(lo, SUBLANES), :] * wk
        for q in groups:
            y_ref[pl.ds(pl.multiple_of(r0 + q * SUBLANES, SUBLANES), SUBLANES), :] = acc[q]
        return carry

    lax.fori_loop(0, tt // rows, chunk, 0)

    acc = y_ref[...]
    mu = jnp.mean(acc, axis=-1, keepdims=True)
    xc = acc - mu
    var = jnp.mean(xc * xc, axis=-1, keepdims=True)
    y = xc * lax.rsqrt(var + EPS) * lg_ref[...] + lb_ref[...]
    o_ref[...] = (y * jax.nn.sigmoid(y)).astype(o_ref.dtype)


def conv_module(proj, conv_w, conv_b, ln_g, ln_b, *, tt=512):
    s = proj.shape[0]
    c = conv_w.shape[1]
    row = lambda v: v.reshape(1, c)
    return pl.pallas_call(
        functools.partial(_conv_kernel, tt=tt),
        grid=(s // tt,),
        in_specs=[
            pl.BlockSpec((tt, c), lambda i: (i, 0)),
            pl.BlockSpec((tt, c), lambda i: (i, 1)),
            pl.BlockSpec((CONV_WIDTH, c), lambda i: (0, 0)),
            pl.BlockSpec((1, c), lambda i: (0, 0)),
            pl.BlockSpec((1, c), lambda i: (0, 0)),
            pl.BlockSpec((1, c), lambda i: (0, 0)),
        ],
        out_specs=pl.BlockSpec((tt, c), lambda i: (i, 0)),
        out_shape=jax.ShapeDtypeStruct((s, c), BF16),
        scratch_shapes=[pltpu.VMEM((SUBLANES, tt + CONV_HALO, c), F32), pltpu.VMEM((tt, c), F32),
                        pltpu.VMEM((CONV_WIDTH, SUBLANES, c), F32)],
        compiler_params=_params("arbitrary"),
        name="conv_module",
    )(proj, proj, conv_w, row(conv_b), row(ln_g), row(ln_b))


def _compress_kernel(kt_ref, pos_ref, w1_ref, w2_ref, o_ref):
    kt = kt_ref[0, 0].astype(F32)
    half = kt.shape[1]
    first = (kt + pos_ref[0, 0:1, :]).astype(BF16)
    second = (kt + pos_ref[0, 1:2, :]).astype(BF16)
    p = jnp.dot(first, w1_ref[0, 0:half, :], preferred_element_type=F32)
    q = jnp.dot(second, w1_ref[0, half:, :], preferred_element_type=F32)
    n = q.shape[0]
    h = p + pltpu.roll(q, n - 1, axis=0)
    h = h * jax.nn.sigmoid(h)
    o_ref[0, 0] = jnp.dot(h.astype(BF16), w2_ref[0], preferred_element_type=F32).astype(o_ref.dtype)


def compress(kt, pos, w1, w2):
    two, g, nch, half = kt.shape
    hid = w1.shape[2]
    dh = w2.shape[2]
    return pl.pallas_call(
        _compress_kernel,
        grid=(two, g),
        in_specs=[
            pl.BlockSpec((1, 1, nch, half), lambda a, b: (a, b, 0, 0)),
            pl.BlockSpec((1, 2, half), lambda a, b: (a, 0, 0)),
            pl.BlockSpec((1, 2 * half, hid), lambda a, b: (a, 0, 0)),
            pl.BlockSpec((1, hid, dh), lambda a, b: (a, 0, 0)),
        ],
        out_specs=pl.BlockSpec((1, 1, nch, dh), lambda a, b: (a, b, 0, 0)),
        out_shape=jax.ShapeDtypeStruct((two, g, nch, dh), BF16),
        compiler_params=_params("parallel", "parallel"),
        name="compress",
    )(kt, pos, w1, w2)


def _split3(x):
    hi = x.astype(BF16)
    r1 = x - hi.astype(F32)
    mid = r1.astype(BF16)
    lo = (r1 - mid.astype(F32)).astype(BF16)
    return hi, mid, lo


def _nsa_kernel(qt_ref, kc_ref, vct_ref, ka_ref, vat_ref, kw_ref, vwt_ref, glt_ref, o_ref,
                qa_ref, m_ref, acc_ref, s_ref, part_ref, *, seq):
    i = pl.program_id(1)
    s0 = i * Q_BLOCK
    dh = NSA_HEAD_DIM
    cols = NSA_GROUP * Q_BLOCK
    pair = 2 * Q_BLOCK
    n_sel = seq // SEL_LEN
    n_cmp = kc_ref.shape[1]

    qt = jnp.concatenate([qt_ref[r] for r in range(NSA_GROUP)], axis=1)
    qt = (qt.astype(F32) * (dh ** -0.5 * LOG2E)).astype(BF16)
    t_col = s0 + (lax.broadcasted_iota(jnp.int32, (1, cols), 1) & (Q_BLOCK - 1))

    sc = jnp.dot(kc_ref[0], qt, preferred_element_type=F32)
    cmp_end = lax.broadcasted_iota(jnp.int32, (n_cmp, 1), 0) * CMP_STRIDE + (CMP_LEN - 1)
    sc = jnp.where(cmp_end <= t_col, sc, NEG_INF)
    e = jnp.exp2(sc - jnp.max(sc, axis=0, keepdims=True))
    l = jnp.sum(e, axis=0, keepdims=True)
    pc = e * jnp.where(t_col >= CMP_LEN - 1, 1.0 / l, 0.0)
    oc = jnp.dot(vct_ref[0], pc.astype(BF16), preferred_element_type=F32)

    wlen = WINDOW + Q_BLOCK
    w0 = pl.multiple_of(jnp.maximum(s0 - WINDOW, 0), Q_BLOCK)
    wc = jnp.maximum(i - WINDOW // Q_BLOCK, 0)
    sw = jnp.dot(kw_ref[0, pl.ds(w0, wlen), :], qt, preferred_element_type=F32)
    wpos = w0 + lax.broadcasted_iota(jnp.int32, (wlen, 1), 0)
    sw = jnp.where(wpos <= t_col, sw, NEG_INF)
    sw = jnp.concatenate([jnp.where(wpos[0:Q_BLOCK] > t_col - WINDOW, sw[0:Q_BLOCK], NEG_INF), sw[Q_BLOCK:]],
                         axis=0)
    pw = jnp.exp2(sw - jnp.max(sw, axis=0, keepdims=True)).astype(BF16)
    accw = jnp.zeros((vwt_ref.shape[2], cols), F32)
    for j in range(wlen // Q_BLOCK):
        accw = accw + jnp.dot(vwt_ref[0, wc + j], pw[j * Q_BLOCK:(j + 1) * Q_BLOCK],
                              preferred_element_type=F32)
    ow = accw[0:dh] * (1.0 / accw[dh:dh + 1])

    def gate_row(b):
        gate = jax.nn.sigmoid(glt_ref[0].astype(F32))
        return jnp.concatenate(
            [gate[N_BRANCH * r + b:N_BRANCH * r + b + 1] for r in range(NSA_GROUP)], axis=1)

    part_ref[...] = gate_row(0) * oc + gate_row(2) * ow

    imp_c = pc[:, 0:Q_BLOCK]
    for r in range(1, NSA_GROUP):
        imp_c = imp_c + pc[:, r * Q_BLOCK:(r + 1) * Q_BLOCK]
    sj = lax.broadcasted_iota(jnp.int32, (n_sel, n_cmp), 0) * SEL_LEN
    cn = lax.broadcasted_iota(jnp.int32, (n_sel, n_cmp), 1) * CMP_STRIDE
    overlap = jnp.where((cn <= sj + SEL_LEN - 1) & (cn + CMP_LEN - 1 >= sj), 1.0, 0.0).astype(BF16)
    imp = jnp.zeros((n_sel, Q_BLOCK), F32)
    for part in _split3(imp_c):
        imp = imp + jnp.dot(overlap, part, preferred_element_type=F32)

    jj = lax.broadcasted_iota(jnp.int32, (n_sel, Q_BLOCK), 0)
    cur = (s0 + lax.broadcasted_iota(jnp.int32, (1, Q_BLOCK), 1)) >> (SEL_LEN.bit_length() - 1)
    forced = (jj == 0) | (jj == cur) | (jj == cur - 1)
    future = jj > cur
    assert FORCE_BONUS > 2 * NSA_GROUP
    n_forced = 3
    work = jnp.where(forced, -jnp.inf, jnp.where(future, NEG_INF, imp))
    jf = jj.astype(F32)
    chosen = jnp.where(forced, 1.0, 0.0)
    for _ in range(min(SEL_TOPK, n_sel) - n_forced):
        mx = jnp.max(work, axis=0, keepdims=True)
        first = jnp.min(jnp.where(work == mx, jf, float(n_sel)), axis=0, keepdims=True)
        hit = jf == first
        chosen = jnp.where(hit, 1.0, chosen)
        work = jnp.where(hit, -jnp.inf, work)
    bias = jnp.where((chosen > 0.0) & jnp.logical_not(future), 0.0, NEG_INF).astype(BF16)
    qa_ref[0:n_sel, :] = jnp.concatenate([bias] * NSA_GROUP, axis=1)
    qa_ref[n_sel:, :] = qt

    m_ref[...] = jnp.full(m_ref.shape, NEG_INF, F32)
    acc_ref[...] = jnp.zeros(acc_ref.shape, F32)

    def scores(c, dst_ref):
        k0 = pl.multiple_of(c * SEL_CHUNK, SEL_CHUNK)
        ka = ka_ref[0, pl.ds(k0, SEL_CHUNK), :]
        for h0 in range(0, cols, pair):
            dst_ref[:, h0:h0 + pair] = jnp.dot(ka, qa_ref[:, h0:h0 + pair], preferred_element_type=F32)

    def accumulate(c, src_ref, causal):
        s = src_ref[...]
        if causal:
            kpos = c * SEL_CHUNK + lax.broadcasted_iota(jnp.int32, (SEL_CHUNK, 1), 0)
            s = jnp.where(kpos <= t_col, s, NEG_INF)
        m_old = m_ref[...]
        m_new = jnp.maximum(m_old, jnp.max(s, axis=0, keepdims=True))
        p = jnp.exp2(s - m_new).astype(BF16)
        pv = jnp.dot(vat_ref[0, c], p, preferred_element_type=F32)
        acc_ref[...] = acc_ref[...] * jnp.exp2(m_old - m_new) + pv
        m_ref[...] = m_new

    n_full = s0 >> (SEL_CHUNK.bit_length() - 1)
    sa_ref, sb_ref = s_ref.at[0], s_ref.at[1]
    scores(0, sa_ref)

    def body(j, carry):
        c = 2 * j
        scores(c + 1, sb_ref)
        accumulate(c, sa_ref, False)
        scores(c + 2, sa_ref)
        accumulate(c + 1, sb_ref, False)
        return carry

    lax.fori_loop(0, n_full >> 1, body, 0)

    @pl.when((n_full & 1) == 1)
    def _():
        scores(n_full, sb_ref)
        accumulate(n_full - 1, sa_ref, False)

    accumulate(n_full, s_ref.at[n_full & 1], True)
    acc = acc_ref[...]
    os_ = acc[0:dh] * (1.0 / acc[dh:dh + 1])

    out_t = part_ref[...] + gate_row(1) * os_
    out_t = jnp.concatenate([out_t[:, r * Q_BLOCK:(r + 1) * Q_BLOCK] for r in range(NSA_GROUP)], axis=0)
    o_ref[...] = out_t.T.astype(o_ref.dtype)


def nsa_attention(qt, kc, vct, k_aug, vat, kw, vwt, glt):
    h, dh, s = qt.shape
    g = kc.shape[0]
    dv = vat.shape[2]
    cols = NSA_GROUP * Q_BLOCK
    whole = lambda a: pl.BlockSpec((1,) + a.shape[1:], lambda gi, i: (gi,) + (0,) * (a.ndim - 1))
    return pl.pallas_call(
        functools.partial(_nsa_kernel, seq=s),
        grid=(g, s // Q_BLOCK),
        in_specs=[
            pl.BlockSpec((NSA_GROUP, dh, Q_BLOCK), lambda gi, i: (gi, 0, i)),
            whole(kc), whole(vct), whole(k_aug), whole(vat), whole(kw), whole(vwt),
            pl.BlockSpec((1, NSA_GROUP * N_BRANCH, Q_BLOCK), lambda gi, i: (gi, 0, i)),
        ],
        out_specs=pl.BlockSpec((Q_BLOCK, NSA_GROUP * dh), lambda gi, i: (i, gi)),
        out_shape=jax.ShapeDtypeStruct((s, h * dh), BF16),
        scratch_shapes=[pltpu.VMEM((k_aug.shape[2], cols), BF16),
                        pltpu.VMEM((1, cols), F32),
                        pltpu.VMEM((dv, cols), F32),
                        pltpu.VMEM((2, SEL_CHUNK, cols), F32),
                        pltpu.VMEM((dh, cols), F32)],
        compiler_params=_params("parallel", "arbitrary"),
        name="nsa_attention",
    )(qt, kc, vct, k_aug, vat, kw, vwt, glt)


def _out_proj_kernel(c_ref, o_ref, w_ref, x_ref, y_ref):
    half = c_ref.shape[1]
    acc = jnp.dot(c_ref[...], w_ref[0:half, :], preferred_element_type=F32)
    acc = acc + jnp.dot(o_ref[...], w_ref[half:, :], preferred_element_type=F32)
    y_ref[...] = x_ref[...] + acc


def out_proj(c, o, w, x, *, tm=1024, tn=512):
    m, half = c.shape
    n = w.shape[1]
    return pl.pallas_call(
        _out_proj_kernel,
        grid=(m // tm, n // tn),
        in_specs=[
            pl.BlockSpec((tm, half), lambda i, j: (i, 0)),
            pl.BlockSpec((tm, half), lambda i, j: (i, 0)),
            pl.BlockSpec((2 * half, tn), lambda i, j: (0, j)),
            pl.BlockSpec((tm, tn), lambda i, j: (i, j)),
        ],
        out_specs=pl.BlockSpec((tm, tn), lambda i, j: (i, j)),
        out_shape=jax.ShapeDtypeStruct((m, n), F32),
        compiler_params=_params("parallel", "parallel"),
        name="out_proj",
    )(c, o, w, x)


def _pool_kernel(x_ref, halo_ref, g_ref, w_ref, sc_ref, o_ref, *, tm):
    i = pl.program_id(0)
    x = x_ref[...]
    xn = _rms(x, g_ref[...])
    hn = _rms(halo_ref[...], g_ref[...]) * jnp.where(i > 0, 1.0, 0.0)
    ext = jnp.concatenate([hn, xn], axis=0)
    pd = w_ref.shape[1]
    pos1 = i * tm + 1 + lax.broadcasted_iota(jnp.int32, (tm, 1), 0)
    for gi, win in enumerate(POOL_WINDOWS):
        sl = slice(gi * pd, (gi + 1) * pd)
        s = ext[:, sl]
        sh = 1
        while sh < win:
            s = s + pltpu.roll(s, sh, axis=0)
            sh *= 2
        cnt = jnp.minimum(pos1, win).astype(F32)
        p = s[POOL_HALO:] / cnt - xn[:, sl]
        y = jnp.dot(p.astype(BF16), w_ref[gi], preferred_element_type=F32)
        o_ref[:, sl] = x[:, sl] + y * sc_ref[:, sl]


def pool_mixer(x, g, pool_w, pool_scale, *, tm=512):
    m, d = x.shape
    ng, pd, _ = pool_w.shape
    hb = tm // POOL_HALO
    return pl.pallas_call(
        functools.partial(_pool_kernel, tm=tm),
        grid=(m // tm,),
        in_specs=[
            pl.BlockSpec((tm, d), lambda i: (i, 0)),
            pl.BlockSpec((POOL_HALO, d), lambda i: (jnp.maximum(i * hb - 1, 0), 0)),
            pl.BlockSpec((1, d), lambda i: (0, 0)),
            pl.BlockSpec((ng, pd, pd), lambda i: (0, 0, 0)),
            pl.BlockSpec((1, d), lambda i: (0, 0)),
        ],
        out_specs=pl.BlockSpec((tm, d), lambda i: (i, 0)),
        out_shape=jax.ShapeDtypeStruct((m, d), F32),
        compiler_params=_params("parallel"),
        name="pool_mixer",
    )(x, x, g.reshape(1, d), pool_w, pool_scale.reshape(1, d))


def _even_mixer(x, g, w_in, w_out, conv_w, conv_b, ln_g, ln_b, pos_k, pos_v, kw1, kw2, vw1, vw2):
    s, d = x.shape
    dh, gk, hq = NSA_HEAD_DIM, NSA_KV_HEADS, NSA_HEADS
    conv_cols = 2 * conv_w.shape[1]
    q_cols, kv_cols = hq * dh, gk * dh
    tn = 512
    w_all = jnp.pad(w_in.astype(BF16), ((0, 0), (0, (-w_in.shape[1]) % tn)))

    proj = norm_matmul(x, g, w_all, out_dtype=BF16, tn=tn)
    c = conv_module(proj, conv_w, conv_b, ln_g, ln_b)
    proj_nsa = proj[:, conv_cols:]

    cut = lambda k: proj_nsa[:, q_cols + k * kv_cols:q_cols + (k + 1) * kv_cols]
    hm = lambda t: t.reshape(s, gk, dh).transpose(1, 0, 2)
    qt = proj_nsa[:, :q_cols].reshape(s, hq, dh).transpose(1, 2, 0)
    k_c, v_c, k_s, v_s, k_w, v_w = (hm(cut(k)) for k in range(6))
    glt = proj_nsa[:, q_cols + 6 * kv_cols:q_cols + 6 * kv_cols + hq * N_BRANCH]
    glt = glt.reshape(s, gk, NSA_GROUP * N_BRANCH).transpose(1, 2, 0)

    nch = s // CMP_STRIDE
    kt = jnp.stack([k_c, v_c]).reshape(2, gk, nch, CMP_STRIDE * dh)
    pos = jnp.stack([pos_k, pos_v]).reshape(2, 2, CMP_STRIDE * dh)
    kvc = compress(kt, pos, jnp.stack([kw1, vw1]).astype(BF16), jnp.stack([kw2, vw2]).astype(BF16))

    n_sel = s // SEL_LEN
    onehot = (jnp.arange(s)[:, None] // SEL_LEN == jnp.arange(n_sel)[None, :]).astype(BF16)
    k_aug = jnp.concatenate([jnp.broadcast_to(onehot, (gk, s, n_sel)), k_s], axis=-1)
    ones = jnp.ones((gk, s, ONES_ROWS), BF16)
    chunk_t = lambda v, n: (jnp.concatenate([v, ones], axis=-1)
                            .reshape(gk, s // n, n, dh + ONES_ROWS).transpose(0, 1, 3, 2))
    o = nsa_attention(qt, kvc[0], kvc[1].transpose(0, 2, 1), k_aug, chunk_t(v_s, SEL_CHUNK),
                      k_w, chunk_t(v_w, Q_BLOCK), glt)
    return out_proj(c, o, w_out.astype(BF16), x)


def kernel(x, mix_norm, mlp_norm, w_mlp_in, w_mlp_out, w_in, w_out, conv_w, conv_b, conv_ln_g,
           conv_ln_b, cmp_pos_k, cmp_pos_v, cmp_k_w1, cmp_k_w2, cmp_v_w1, cmp_v_w2, pool_w,
           pool_scale, final_norm):
    b, s, d = x.shape
    depth = mix_norm.shape[0]
    outs = []
    for bi in range(b):
        h = x[bi]
        for layer in range(depth):
            i = layer // 2
            if layer % 2 == 0:
                h = _even_mixer(h, mix_norm[layer], w_in[i], w_out[i], conv_w[i], conv_b[i],
                                conv_ln_g[i], conv_ln_b[i], cmp_pos_k[i], cmp_pos_v[i],
                                cmp_k_w1[i], cmp_k_w2[i], cmp_v_w1[i], cmp_v_w2[i])
            else:
                h = pool_mixer(h, mix_norm[layer], pool_w[i].astype(BF16), pool_scale[i])
            fg = final_norm if layer == depth - 1 else None
            h = fused_mlp(h, mlp_norm[layer], w_mlp_in, w_mlp_out, layer, fg)
        outs.append(h)
    return jnp.stack(outs)
```

```python
import functools

import jax
import jax.numpy as jnp
from jax import lax
from jax.experimental import pallas as pl
from jax.experimental.pallas import tpu as pltpu

F32 = jnp.float32
BF16 = jnp.bfloat16

EPS = 1e-6
NEG_INF = -1e30
LOG2E = 1.4426950408889634
CONV_WIDTH = 31
SUBLANES = 8
CONV_HALO = 32
NSA_HEADS = 16
NSA_HEAD_DIM = 64
NSA_KV_HEADS = 4
NSA_GROUP = NSA_HEADS // NSA_KV_HEADS
N_BRANCH = 3
CMP_LEN = 32
CMP_STRIDE = 16
SEL_LEN = 64
SEL_TOPK = 16
WINDOW = 512
Q_BLOCK = 256
FORCE_BONUS = 1e3
POOL_WINDOWS = (2, 4, 8, 16)
POOL_HALO = 16
SEL_CHUNK = 512
ONES_ROWS = 16

VMEM_LIMIT = 56 * 1024 * 1024


def _params(*sem):
    return pltpu.CompilerParams(dimension_semantics=sem, vmem_limit_bytes=VMEM_LIMIT)


def _rms(x, g):
    r = lax.rsqrt(jnp.mean(x * x, axis=-1, keepdims=True) + EPS)
    return x * r * g


def _norm_mm_kernel(x_ref, g_ref, w_ref, o_ref, xn_ref, *, act):
    @pl.when(pl.program_id(1) == 0)
    def _():
        xn_ref[...] = _rms(x_ref[...], g_ref[...]).astype(BF16)

    acc = jnp.dot(xn_ref[...], w_ref[...], preferred_element_type=F32)
    if act:
        acc = jnp.square(jnp.maximum(acc, 0.0))
    o_ref[...] = acc.astype(o_ref.dtype)


def norm_matmul(x, g, w, *, out_dtype, act=False, tm=1024, tn=512):
    m, k = x.shape
    n = w.shape[1]
    assert m % tm == 0 and n % tn == 0
    return pl.pallas_call(
        functools.partial(_norm_mm_kernel, act=act),
        grid=(m // tm, n // tn),
        in_specs=[
            pl.BlockSpec((tm, k), lambda i, j: (i, 0)),
            pl.BlockSpec((1, k), lambda i, j: (0, 0)),
            pl.BlockSpec((k, tn), lambda i, j: (0, j)),
        ],
        out_specs=pl.BlockSpec((tm, tn), lambda i, j: (i, j)),
        out_shape=jax.ShapeDtypeStruct((m, n), out_dtype),
        scratch_shapes=[pltpu.VMEM((tm, k), BF16)],
        compiler_params=_params("parallel", "arbitrary"),
        name="norm_matmul",
    )(x, g.reshape(1, k), w)


def _mlp_kernel(x_ref, g_ref, w1_ref, w2_ref, gf_ref, o_ref, xn_ref, h_ref, *, final_norm):
    f = pl.program_id(1)

    @pl.when(f == 0)
    def _():
        x = x_ref[...]
        xn_ref[...] = _rms(x, g_ref[...]).astype(BF16)
        o_ref[...] = x
        h_ref[...] = jnp.zeros(h_ref.shape, BF16)

    h_new = jnp.dot(xn_ref[...], w1_ref[...].astype(BF16), preferred_element_type=F32)
    o_ref[...] += jnp.dot(h_ref[...], w2_ref[...].astype(BF16), preferred_element_type=F32)
    h_ref[...] = jnp.square(jnp.maximum(h_new, 0.0)).astype(BF16)

    if final_norm:
        @pl.when(f == pl.num_programs(1) - 1)
        def _():
            o_ref[...] = _rms(o_ref[...], gf_ref[...])


def fused_mlp(x, g, w1, w2, layer, gf=None, *, tm=2048, tf=256):
    m, d = x.shape
    nf = w1.shape[2] // tf
    final_norm = gf is not None
    if gf is None:
        gf = g
    once = pl.Buffered(1)
    return pl.pallas_call(
        functools.partial(_mlp_kernel, final_norm=final_norm),
        grid=(m // tm, nf + 1),
        in_specs=[
            pl.BlockSpec((tm, d), lambda i, f: (i, 0), pipeline_mode=once),
            pl.BlockSpec((1, d), lambda i, f: (0, 0)),
            pl.BlockSpec((None, d, tf), lambda i, f: (layer, 0, jnp.minimum(f, nf - 1))),
            pl.BlockSpec((None, tf, d), lambda i, f: (layer, jnp.maximum(f - 1, 0), 0)),
            pl.BlockSpec((1, d), lambda i, f: (0, 0)),
        ],
        out_specs=pl.BlockSpec((tm, d), lambda i, f: (i, 0), pipeline_mode=once),
        out_shape=jax.ShapeDtypeStruct((m, d), F32),
        scratch_shapes=[pltpu.VMEM((tm, d), BF16), pltpu.VMEM((tm, tf), BF16)],
        compiler_params=_params("parallel", "arbitrary"),
        name="fused_mlp",
    )(x, g.reshape(1, d), w1, w2, gf.reshape(1, d))


def _conv_kernel(a_ref, gt_ref, w_ref, b_ref, lg_ref, lb_ref, o_ref, cs_ref, y_ref, wb_ref, *, tt):
    i = pl.program_id(0)

    @pl.when(i == 0)
    def _():
        cs_ref[0, 0:CONV_HALO, :] = jnp.zeros((CONV_HALO, cs_ref.shape[2]), F32)
        for k in range(CONV_WIDTH):
            wb_ref[k] = jnp.broadcast_to(w_ref[k:k + 1, :], wb_ref.shape[1:])

    @pl.when(i > 0)
    def _():
        cs_ref[0, 0:CONV_HALO, :] = cs_ref[0, tt:tt + CONV_HALO, :]

    cs_ref[0, CONV_HALO:, :] = a_ref[...].astype(F32) * jax.nn.sigmoid(gt_ref[...].astype(F32))
    c_ext = cs_ref[0]
    for b in range(1, SUBLANES):
        cs_ref[b] = pltpu.roll(c_ext, b, axis=0)

    rows = 4 * SUBLANES

    def chunk(j, carry):
        r0 = pl.multiple_of(j * rows, rows)
        groups = range(rows // SUBLANES)
        acc = [jnp.zeros((SUBLANES, cs_ref.shape[2]), F32) + b_ref[...] for _ in groups]
        for k in range(CONV_WIDTH):
            a, b = divmod(k, SUBLANES)
            wk = wb_ref[CONV_WIDTH - 1 - k]
            for q in groups:
                lo = pl.multiple_of(r0 + CONV_HALO + (q - a) * SUBLANES, SUBLANES)
                acc[q] = acc[q] + cs_ref[b, pl.ds(lo, SUBLANES), :] * wk
        for q in groups:
            y_ref[pl.ds(pl.multiple_of(r0 + q * SUBLANES, SUBLANES), SUBLANES), :] = acc[q]
        return carry

    lax.fori_loop(0, tt // rows, chunk, 0)

    acc = y_ref[...]
    mu = jnp.mean(acc, axis=-1, keepdims=True)
    xc = acc - mu
    var = jnp.mean(xc * xc, axis=-1, keepdims=True)
    y = xc * lax.rsqrt(var + EPS) * lg_ref[...] + lb_ref[...]
    o_ref[...] = (y * jax.nn.sigmoid(y)).astype(o_ref.dtype)


def conv_module(proj, conv_w, conv_b, ln_g, ln_b, *, tt=512):
    s = proj.shape[0]
    c = conv_w.shape[1]
    row = lambda v: v.reshape(1, c)
    return pl.pallas_call(
        functools.partial(_conv_kernel, tt=tt),
        grid=(s // tt,),
        in_specs=[
            pl.BlockSpec((tt, c), lambda i: (i, 0)),
            pl.BlockSpec((tt, c), lambda i: (i, 1)),
            pl.BlockSpec((CONV_WIDTH, c), lambda i: (0, 0)),
            pl.BlockSpec((1, c), lambda i: (0, 0)),
            pl.BlockSpec((1, c), lambda i: (0, 0)),
            pl.BlockSpec((1, c), lambda i: (0, 0)),
        ],
        out_specs=pl.BlockSpec((tt, c), lambda i: (i, 0)),
        out_shape=jax.ShapeDtypeStruct((s, c), BF16),
        scratch_shapes=[pltpu.VMEM((SUBLANES, tt + CONV_HALO, c), F32), pltpu.VMEM((tt, c), F32),
                        pltpu.VMEM((CONV_WIDTH, SUBLANES, c), F32)],
        compiler_params=_params("arbitrary"),
        name="conv_module",
    )(proj, proj, conv_w, row(conv_b), row(ln_g), row(ln_b))


def _compress_kernel(kt_ref, pos_ref, w1_ref, w2_ref, o_ref):
    kt = kt_ref[0, 0].astype(F32)
    half = kt.shape[1]
    first = (kt + pos_ref[0, 0:1, :]).astype(BF16)
    second = (kt + pos_ref[0, 1:2, :]).astype(BF16)
    p = jnp.dot(first, w1_ref[0, 0:half, :], preferred_element_type=F32)
    q = jnp.dot(second, w1_ref[0, half:, :], preferred_element_type=F32)
    n = q.shape[0]
    h = p + pltpu.roll(q, n - 1, axis=0)
    h = h * jax.nn.sigmoid(h)
    o_ref[0, 0] = jnp.dot(h.astype(BF16), w2_ref[0], preferred_element_type=F32).astype(o_ref.dtype)


def compress(kt, pos, w1, w2):
    two, g, nch, half = kt.shape
    hid = w1.shape[2]
    dh = w2.shape[2]
    return pl.pallas_call(
        _compress_kernel,
        grid=(two, g),
        in_specs=[
            pl.BlockSpec((1, 1, nch, half), lambda a, b: (a, b, 0, 0)),
            pl.BlockSpec((1, 2, half), lambda a, b: (a, 0, 0)),
            pl.BlockSpec((1, 2 * half, hid), lambda a, b: (a, 0, 0)),
            pl.BlockSpec((1, hid, dh), lambda a, b: (a, 0, 0)),
        ],
        out_specs=pl.BlockSpec((1, 1, nch, dh), lambda a, b: (a, b, 0, 0)),
        out_shape=jax.ShapeDtypeStruct((two, g, nch, dh), BF16),
        compiler_params=_params("parallel", "parallel"),
        name="compress",
    )(kt, pos, w1, w2)


def _split3(x):
    hi = x.astype(BF16)
    r1 = x - hi.astype(F32)
    mid = r1.astype(BF16)
    lo = (r1 - mid.astype(F32)).astype(BF16)
    return hi, mid, lo


def _nsa_kernel(qt_ref, kc_ref, vct_ref, oh_ref, ks_ref, vat_ref, kw_ref, vwt_ref, glt_ref, o_ref,
                qa_ref, m_ref, acc_ref, s_ref, part_ref, *, seq):
    i = pl.program_id(1)
    s0 = i * Q_BLOCK
    dh = NSA_HEAD_DIM
    cols = NSA_GROUP * Q_BLOCK
    pair = 2 * Q_BLOCK
    n_sel = seq // SEL_LEN
    n_cmp = kc_ref.shape[1]

    qt = jnp.concatenate([qt_ref[r] for r in range(NSA_GROUP)], axis=1)
    qt = (qt.astype(F32) * (dh ** -0.5 * LOG2E)).astype(BF16)
    t_col = s0 + (lax.broadcasted_iota(jnp.int32, (1, cols), 1) & (Q_BLOCK - 1))

    sc = jnp.dot(kc_ref[0], qt, preferred_element_type=F32)
    cmp_end = lax.broadcasted_iota(jnp.int32, (n_cmp, 1), 0) * CMP_STRIDE + (CMP_LEN - 1)
    sc = jnp.where(cmp_end <= t_col, sc, NEG_INF)
    e = jnp.exp2(sc - jnp.max(sc, axis=0, keepdims=True))
    l = jnp.sum(e, axis=0, keepdims=True)
    pc = e * jnp.where(t_col >= CMP_LEN - 1, 1.0 / l, 0.0)
    oc = jnp.dot(vct_ref[0], pc.astype(BF16), preferred_element_type=F32)

    imp_c = pc[:, 0:Q_BLOCK]
    for r in range(1, NSA_GROUP):
        imp_c = imp_c + pc[:, r * Q_BLOCK:(r + 1) * Q_BLOCK]
    sj = lax.broadcasted_iota(jnp.int32, (n_sel, n_cmp), 0) * SEL_LEN
    cn = lax.broadcasted_iota(jnp.int32, (n_sel, n_cmp), 1) * CMP_STRIDE
    overlap = jnp.where((cn <= sj + SEL_LEN - 1) & (cn + CMP_LEN - 1 >= sj), 1.0, 0.0).astype(BF16)
    imp = jnp.zeros((n_sel, Q_BLOCK), F32)
    for part in _split3(imp_c):
        imp = imp + jnp.dot(overlap, part, preferred_element_type=F32)

    t_win = t_col + (jnp.min(imp, keepdims=True) < 0.0).astype(jnp.int32)
    wlen = WINDOW + Q_BLOCK
    w0 = pl.multiple_of(jnp.maximum(s0 - WINDOW, 0), Q_BLOCK)
    wc = jnp.maximum(i - WINDOW // Q_BLOCK, 0)
    sw = jnp.dot(kw_ref[0, pl.ds(w0, wlen), :], qt, preferred_element_type=F32)
    wpos = w0 + lax.broadcasted_iota(jnp.int32, (wlen, 1), 0)
    sw = jnp.where(wpos <= t_win, sw, NEG_INF)
    sw = jnp.concatenate([jnp.where(wpos[0:Q_BLOCK] > t_win - WINDOW, sw[0:Q_BLOCK], NEG_INF), sw[Q_BLOCK:]],
                         axis=0)
    pw = jnp.exp2((sw - jnp.max(sw, axis=0, keepdims=True)).astype(BF16))
    accw = jnp.zeros((vwt_ref.shape[2], cols), F32)
    for j in range(wlen // Q_BLOCK):
        accw = accw + jnp.dot(vwt_ref[0, wc + j], pw[j * Q_BLOCK:(j + 1) * Q_BLOCK],
                              preferred_element_type=F32)
    ow = accw[0:dh] * (1.0 / accw[dh:dh + 1])

    def gate_row(b):
        gate = jax.nn.sigmoid(glt_ref[0].astype(F32))
        return jnp.concatenate(
            [gate[N_BRANCH * r + b:N_BRANCH * r + b + 1] for r in range(NSA_GROUP)], axis=1)

    part_ref[...] = gate_row(0) * oc + gate_row(2) * ow


    jj = lax.broadcasted_iota(jnp.int32, (n_sel, Q_BLOCK), 0)
    cur = (s0 + lax.broadcasted_iota(jnp.int32, (1, Q_BLOCK), 1)) >> (SEL_LEN.bit_length() - 1)
    forced = (jj == 0) | (jj == cur) | (jj == cur - 1)
    future = jj > cur
    assert FORCE_BONUS > 2 * NSA_GROUP
    n_forced = 3
    work = jnp.where(forced, -jnp.inf, jnp.where(future, NEG_INF, imp))
    jf = jj.astype(F32)
    chosen = jnp.where(forced, 1.0, 0.0)
    for _ in range(min(SEL_TOPK, n_sel) - n_forced):
        mx = jnp.max(work, axis=0, keepdims=True)
        first = jnp.min(jnp.where(work == mx, jf, float(n_sel)), axis=0, keepdims=True)
        hit = jf == first
        chosen = jnp.where(hit, 1.0, chosen)
        work = jnp.where(hit, -jnp.inf, work)
    bias = jnp.where((chosen > 0.0) & jnp.logical_not(future), 0.0, NEG_INF).astype(BF16)
    qa_ref[0:n_sel, :] = jnp.concatenate([bias] * NSA_GROUP, axis=1)
    qa_ref[n_sel:, :] = qt

    m_ref[...] = jnp.full(m_ref.shape, NEG_INF, F32)
    acc_ref[...] = jnp.zeros(acc_ref.shape, F32)

    def scores(c, dst_ref):
        k0 = pl.multiple_of(c * SEL_CHUNK, SEL_CHUNK)
        ka = jnp.concatenate([oh_ref[pl.ds(k0, SEL_CHUNK), :], ks_ref[0, pl.ds(k0, SEL_CHUNK), :]],
                             axis=1)
        for h0 in range(0, cols, pair):
            dst_ref[:, h0:h0 + pair] = jnp.dot(ka, qa_ref[:, h0:h0 + pair], preferred_element_type=F32)

    def accumulate(c, src_ref, causal):
        s = src_ref[...]
        if causal:
            kpos = c * SEL_CHUNK + lax.broadcasted_iota(jnp.int32, (SEL_CHUNK, 1), 0)
            s = jnp.where(kpos <= t_col, s, NEG_INF)
        m_old = m_ref[...]
        m_new = jnp.maximum(m_old, jnp.max(s, axis=0, keepdims=True))
        p = jnp.exp2(s - m_new).astype(BF16)
        pv = jnp.dot(vat_ref[0, c], p, preferred_element_type=F32)
        acc_ref[...] = acc_ref[...] * jnp.exp2(m_old - m_new) + pv
        m_ref[...] = m_new

    n_full = s0 >> (SEL_CHUNK.bit_length() - 1)
    sa_ref, sb_ref = s_ref.at[0], s_ref.at[1]
    scores(0, sa_ref)

    def body(j, carry):
        c = 2 * j
        scores(c + 1, sb_ref)
        accumulate(c, sa_ref, False)
        scores(c + 2, sa_ref)
        accumulate(c + 1, sb_ref, False)
        return carry

    lax.fori_loop(0, n_full >> 1, body, 0)

    @pl.when((n_full & 1) == 1)
    def _():
        scores(n_full, sb_ref)
        accumulate(n_full - 1, sa_ref, False)

    accumulate(n_full, s_ref.at[n_full & 1], True)
    acc = acc_ref[...]
    os_ = acc[0:dh] * (1.0 / acc[dh:dh + 1])

    out_t = part_ref[...] + gate_row(1) * os_
    out_t = jnp.concatenate([out_t[:, r * Q_BLOCK:(r + 1) * Q_BLOCK] for r in range(NSA_GROUP)], axis=0)
    o_ref[...] = out_t.T.astype(o_ref.dtype)


def nsa_attention(qt, kc, vct, onehot, ks, vat, kw, vwt, glt):
    h, dh, s = qt.shape
    g = kc.shape[0]
    dv = vat.shape[2]
    cols = NSA_GROUP * Q_BLOCK
    whole = lambda a: pl.BlockSpec((1,) + a.shape[1:], lambda gi, i: (gi,) + (0,) * (a.ndim - 1))
    return pl.pallas_call(
        functools.partial(_nsa_kernel, seq=s),
        grid=(g, s // Q_BLOCK),
        in_specs=[
            pl.BlockSpec((NSA_GROUP, dh, Q_BLOCK), lambda gi, i: (gi, 0, i)),
            whole(kc), whole(vct), pl.BlockSpec(onehot.shape, lambda gi, i: (0, 0)),
            whole(ks), whole(vat), whole(kw), whole(vwt),
            pl.BlockSpec((1, NSA_GROUP * N_BRANCH, Q_BLOCK), lambda gi, i: (gi, 0, i)),
        ],
        out_specs=pl.BlockSpec((Q_BLOCK, NSA_GROUP * dh), lambda gi, i: (i, gi)),
        out_shape=jax.ShapeDtypeStruct((s, h * dh), BF16),
        scratch_shapes=[pltpu.VMEM((onehot.shape[1] + dh, cols), BF16),
                        pltpu.VMEM((1, cols), F32),
                        pltpu.VMEM((dv, cols), F32),
                        pltpu.VMEM((2, SEL_CHUNK, cols), F32),
                        pltpu.VMEM((dh, cols), F32)],
        compiler_params=_params("parallel", "arbitrary"),
        name="nsa_attention",
    )(qt, kc, vct, onehot, ks, vat, kw, vwt, glt)


def _out_proj_kernel(c_ref, o_ref, w_ref, x_ref, y_ref):
    half = c_ref.shape[1]
    acc = jnp.dot(c_ref[...], w_ref[0:half, :], preferred_element_type=F32)
    acc = acc + jnp.dot(o_ref[...], w_ref[half:, :], preferred_element_type=F32)
    y_ref[...] = x_ref[...] + acc


def out_proj(c, o, w, x, *, tm=1024, tn=512):
    m, half = c.shape
    n = w.shape[1]
    return pl.pallas_call(
        _out_proj_kernel,
        grid=(m // tm, n // tn),
        in_specs=[
            pl.BlockSpec((tm, half), lambda i, j: (i, 0)),
            pl.BlockSpec((tm, half), lambda i, j: (i, 0)),
            pl.BlockSpec((2 * half, tn), lambda i, j: (0, j)),
            pl.BlockSpec((tm, tn), lambda i, j: (i, j)),
        ],
        out_specs=pl.BlockSpec((tm, tn), lambda i, j: (i, j)),
        out_shape=jax.ShapeDtypeStruct((m, n), F32),
        compiler_params=_params("parallel", "parallel"),
        name="out_proj",
    )(c, o, w, x)


def _pool_kernel(x_ref, halo_ref, g_ref, w_ref, sc_ref, o_ref, *, tm):
    i = pl.program_id(0)
    x = x_ref[...]
    xn = _rms(x, g_ref[...])
    hn = _rms(halo_ref[...], g_ref[...]) * jnp.where(i > 0, 1.0, 0.0)
    ext = jnp.concatenate([hn, xn], axis=0)
    pd = w_ref.shape[1]
    pos1 = i * tm + 1 + lax.broadcasted_iota(jnp.int32, (tm, 1), 0)
    for gi, win in enumerate(POOL_WINDOWS):
        sl = slice(gi * pd, (gi + 1) * pd)
        s = ext[:, sl]
        sh = 1
        while sh < win:
            s = s + pltpu.roll(s, sh, axis=0)
            sh *= 2
        cnt = jnp.minimum(pos1, win).astype(F32)
        p = s[POOL_HALO:] / cnt - xn[:, sl]
        y = jnp.dot(p.astype(BF16), w_ref[gi], preferred_element_type=F32)
        o_ref[:, sl] = x[:, sl] + y * sc_ref[:, sl]


def pool_mixer(x, g, pool_w, pool_scale, *, tm=512):
    m, d = x.shape
    ng, pd, _ = pool_w.shape
    hb = tm // POOL_HALO
    return pl.pallas_call(
        functools.partial(_pool_kernel, tm=tm),
        grid=(m // tm,),
        in_specs=[
            pl.BlockSpec((tm, d), lambda i: (i, 0)),
            pl.BlockSpec((POOL_HALO, d), lambda i: (jnp.maximum(i * hb - 1, 0), 0)),
            pl.BlockSpec((1, d), lambda i: (0, 0)),
            pl.BlockSpec((ng, pd, pd), lambda i: (0, 0, 0)),
            pl.BlockSpec((1, d), lambda i: (0, 0)),
        ],
        out_specs=pl.BlockSpec((tm, d), lambda i: (i, 0)),
        out_shape=jax.ShapeDtypeStruct((m, d), F32),
        compiler_params=_params("parallel"),
        name="pool_mixer",
    )(x, x, g.reshape(1, d), pool_w, pool_scale.reshape(1, d))


def _even_mixer(x, g, w_in, w_out, conv_w, conv_b, ln_g, ln_b, pos_k, pos_v, kw1, kw2, vw1, vw2):
    s, d = x.shape
    dh, gk, hq = NSA_HEAD_DIM, NSA_KV_HEADS, NSA_HEADS
    conv_cols = 2 * conv_w.shape[1]
    q_cols, kv_cols = hq * dh, gk * dh
    tn = 512
    w_all = jnp.pad(w_in.astype(BF16), ((0, 0), (0, (-w_in.shape[1]) % tn)))

    proj = norm_matmul(x, g, w_all, out_dtype=BF16, tn=tn)
    c = conv_module(proj, conv_w, conv_b, ln_g, ln_b)
    proj_nsa = proj[:, conv_cols:]

    cut = lambda k: proj_nsa[:, q_cols + k * kv_cols:q_cols + (k + 1) * kv_cols]
    hm = lambda t: t.reshape(s, gk, dh).transpose(1, 0, 2)
    qt = proj_nsa[:, :q_cols].reshape(s, hq, dh).transpose(1, 2, 0)
    k_c, v_c, k_s, v_s, k_w, v_w = (hm(cut(k)) for k in range(6))
    glt = proj_nsa[:, q_cols + 6 * kv_cols:q_cols + 6 * kv_cols + hq * N_BRANCH]
    glt = glt.reshape(s, gk, NSA_GROUP * N_BRANCH).transpose(1, 2, 0)

    nch = s // CMP_STRIDE
    kt = jnp.stack([k_c, v_c]).reshape(2, gk, nch, CMP_STRIDE * dh)
    pos = jnp.stack([pos_k, pos_v]).reshape(2, 2, CMP_STRIDE * dh)
    kvc = compress(kt, pos, jnp.stack([kw1, vw1]).astype(BF16), jnp.stack([kw2, vw2]).astype(BF16))

    n_sel = s // SEL_LEN
    onehot = (jnp.arange(s)[:, None] // SEL_LEN == jnp.arange(n_sel)[None, :]).astype(BF16)
    ones = jnp.ones((gk, s, ONES_ROWS), BF16)
    chunk_t = lambda v, n: (jnp.concatenate([v, ones], axis=-1)
                            .reshape(gk, s // n, n, dh + ONES_ROWS).transpose(0, 1, 3, 2))
    o = nsa_attention(qt, kvc[0], kvc[1].transpose(0, 2, 1), onehot, k_s, chunk_t(v_s, SEL_CHUNK),
                      k_w, chunk_t(v_w, Q_BLOCK), glt)
    return out_proj(c, o, w_out.astype(BF16), x)


def kernel(x, mix_norm, mlp_norm, w_mlp_in, w_mlp_out, w_in, w_out, conv_w, conv_b, conv_ln_g,
           conv_ln_b, cmp_pos_k, cmp_pos_v, cmp_k_w1, cmp_k_w2, cmp_v_w1, cmp_v_w2, pool_w,
           pool_scale, final_norm):
    b, s, d = x.shape
    depth = mix_norm.shape[0]
    outs = []
    for bi in range(b):
        h = x[bi]
        for layer in range(depth):
            i = layer // 2
            if layer % 2 == 0:
                h = _even_mixer(h, mix_norm[layer], w_in[i], w_out[i], conv_w[i], conv_b[i],
                                conv_ln_g[i], conv_ln_b[i], cmp_pos_k[i], cmp_pos_v[i],
                                cmp_k_w1[i], cmp_k_w2[i], cmp_v_w1[i], cmp_v_w2[i])
            else:
                h = pool_mixer(h, mix_norm[layer], pool_w[i].astype(BF16), pool_scale[i])
            fg = final_norm if layer == depth - 1 else None
            h = fused_mlp(h, mlp_norm[layer], w_mlp_in, w_mlp_out, layer, fg)
        outs.append(h)
    return jnp.stack(outs)
```

```python
import functools

import jax
import jax.numpy as jnp
from jax import lax
from jax.experimental import pallas as pl
from jax.experimental.pallas import tpu as pltpu

F32 = jnp.float32
BF16 = jnp.bfloat16

EPS = 1e-6
NEG_INF = -1e30
LOG2E = 1.4426950408889634
CONV_WIDTH = 31
SUBLANES = 8
CONV_HALO = 32
NSA_HEADS = 16
NSA_HEAD_DIM = 64
NSA_KV_HEADS = 4
NSA_GROUP = NSA_HEADS // NSA_KV_HEADS
N_BRANCH = 3
CMP_LEN = 32
CMP_STRIDE = 16
SEL_LEN = 64
SEL_TOPK = 16
WINDOW = 512
Q_BLOCK = 256
FORCE_BONUS = 1e3
POOL_WINDOWS = (2, 4, 8, 16)
POOL_HALO = 16
SEL_CHUNK = 512
ONES_ROWS = 16
T_CHUNK = 256
GATE_SLOT = 16

VMEM_LIMIT = 56 * 1024 * 1024


def _params(*sem):
    return pltpu.CompilerParams(dimension_semantics=sem, vmem_limit_bytes=VMEM_LIMIT)


def _rms(x, g):
    r = lax.rsqrt(jnp.mean(x * x, axis=-1, keepdims=True) + EPS)
    return x * r * g


def _in_proj_kernel(x_ref, g_ref, w_ref, on_ref, ot_ref, xn_ref, *, n_nat):
    j = pl.program_id(1)

    @pl.when(j == 0)
    def _():
        xn_ref[...] = _rms(x_ref[...], g_ref[...]).astype(BF16)

    acc = jnp.dot(xn_ref[...], w_ref[...], preferred_element_type=F32)

    @pl.when(j < n_nat)
    def _():
        on_ref[...] = acc.astype(on_ref.dtype)

    @pl.when(j >= n_nat)
    def _():
        acc_t = acc.T
        for k in range(ot_ref.shape[0]):
            ot_ref[k] = acc_t[:, k * T_CHUNK:(k + 1) * T_CHUNK].astype(ot_ref.dtype)


def in_projection(x, g, w, n_nat, *, tm=1024, tn=512):
    m, k = x.shape
    n_t = w.shape[1] // tn - n_nat
    assert w.shape[1] % tn == 0 and m % tm == 0 and tm % T_CHUNK == 0
    return pl.pallas_call(
        functools.partial(_in_proj_kernel, n_nat=n_nat),
        grid=(m // tm, n_nat + n_t),
        in_specs=[
            pl.BlockSpec((tm, k), lambda i, j: (i, 0)),
            pl.BlockSpec((1, k), lambda i, j: (0, 0)),
            pl.BlockSpec((k, tn), lambda i, j: (0, j)),
        ],
        out_specs=[
            pl.BlockSpec((tm, tn), lambda i, j: (i, jnp.minimum(j, n_nat - 1))),
            pl.BlockSpec((tm // T_CHUNK, tn, T_CHUNK), lambda i, j: (i, jnp.maximum(j - n_nat, 0), 0)),
        ],
        out_shape=[jax.ShapeDtypeStruct((m, n_nat * tn), BF16),
                   jax.ShapeDtypeStruct((m // T_CHUNK, n_t * tn, T_CHUNK), BF16)],
        scratch_shapes=[pltpu.VMEM((tm, k), BF16)],
        compiler_params=_params("parallel", "arbitrary"),
        name="in_projection",
    )(x, g.reshape(1, k), w)


def _mlp_kernel(x_ref, g_ref, w1_ref, w2_ref, gf_ref, o_ref, xn_ref, h_ref, *, final_norm):
    f = pl.program_id(1)

    @pl.when(f == 0)
    def _():
        x = x_ref[...]
        xn_ref[...] = _rms(x, g_ref[...]).astype(BF16)
        o_ref[...] = x
        h_ref[...] = jnp.zeros(h_ref.shape, BF16)

    h_new = jnp.dot(xn_ref[...], w1_ref[...].astype(BF16), preferred_element_type=F32)
    o_ref[...] += jnp.dot(h_ref[...], w2_ref[...].astype(BF16), preferred_element_type=F32)
    h_ref[...] = jnp.square(jnp.maximum(h_new, 0.0)).astype(BF16)

    if final_norm:
        @pl.when(f == pl.num_programs(1) - 1)
        def _():
            o_ref[...] = _rms(o_ref[...], gf_ref[...])


def fused_mlp(x, g, w1, w2, layer, gf=None, *, tm=2048, tf=256):
    m, d = x.shape
    nf = w1.shape[2] // tf
    final_norm = gf is not None
    if gf is None:
        gf = g
    once = pl.Buffered(1)
    return pl.pallas_call(
        functools.partial(_mlp_kernel, final_norm=final_norm),
        grid=(m // tm, nf + 1),
        in_specs=[
            pl.BlockSpec((tm, d), lambda i, f: (i, 0), pipeline_mode=once),
            pl.BlockSpec((1, d), lambda i, f: (0, 0)),
            pl.BlockSpec((None, d, tf), lambda i, f: (layer, 0, jnp.minimum(f, nf - 1))),
            pl.BlockSpec((None, tf, d), lambda i, f: (layer, jnp.maximum(f - 1, 0), 0)),
            pl.BlockSpec((1, d), lambda i, f: (0, 0)),
        ],
        out_specs=pl.BlockSpec((tm, d), lambda i, f: (i, 0), pipeline_mode=once),
        out_shape=jax.ShapeDtypeStruct((m, d), F32),
        scratch_shapes=[pltpu.VMEM((tm, d), BF16), pltpu.VMEM((tm, tf), BF16)],
        compiler_params=_params("parallel", "arbitrary"),
        name="fused_mlp",
    )(x, g.reshape(1, d), w1, w2, gf.reshape(1, d))


def _conv_kernel(a_ref, gt_ref, w_ref, b_ref, lg_ref, lb_ref, o_ref, cs_ref, y_ref, wb_ref, *, tt):
    i = pl.program_id(0)

    @pl.when(i == 0)
    def _():
        cs_ref[0, 0:CONV_HALO, :] = jnp.zeros((CONV_HALO, cs_ref.shape[2]), F32)
        for k in range(CONV_WIDTH):
            wb_ref[k] = jnp.broadcast_to(w_ref[k:k + 1, :], wb_ref.shape[1:])

    @pl.when(i > 0)
    def _():
        cs_ref[0, 0:CONV_HALO, :] = cs_ref[0, tt:tt + CONV_HALO, :]

    cs_ref[0, CONV_HALO:, :] = a_ref[...].astype(F32) * jax.nn.sigmoid(gt_ref[...].astype(F32))
    c_ext = cs_ref[0]
    for b in range(1, SUBLANES):
        cs_ref[b] = pltpu.roll(c_ext, b, axis=0)

    rows = 4 * SUBLANES

    def chunk(j, carry):
        r0 = pl.multiple_of(j * rows, rows)
        groups = range(rows // SUBLANES)
        acc = [jnp.zeros((SUBLANES, cs_ref.shape[2]), F32) + b_ref[...] for _ in groups]
        for k in range(CONV_WIDTH):
            a, b = divmod(k, SUBLANES)
            wk = wb_ref[CONV_WIDTH - 1 - k]
            for q in groups:
                lo = pl.multiple_of(r0 + CONV_HALO + (q - a) * SUBLANES, SUBLANES)
                acc[q] = acc[q] + cs_ref[b, pl.ds(lo, SUBLANES), :] * wk
        for q in groups:
            y_ref[pl.ds(pl.multiple_of(r0 + q * SUBLANES, SUBLANES), SUBLANES), :] = acc[q]
        return carry

    lax.fori_loop(0, tt // rows, chunk, 0)

    acc = y_ref[...]
    mu = jnp.mean(acc, axis=-1, keepdims=True)
    xc = acc - mu
    var = jnp.mean(xc * xc, axis=-1, keepdims=True)
    y = xc * lax.rsqrt(var + EPS) * lg_ref[...] + lb_ref[...]
    o_ref[...] = (y * jax.nn.sigmoid(y)).astype(o_ref.dtype)


def conv_module(proj, conv_w, conv_b, ln_g, ln_b, *, tt=512):
    s = proj.shape[0]
    c = conv_w.shape[1]
    row = lambda v: v.reshape(1, c)
    return pl.pallas_call(
        functools.partial(_conv_kernel, tt=tt),
        grid=(s // tt,),
        in_specs=[
            pl.BlockSpec((tt, c), lambda i: (i, 0)),
            pl.BlockSpec((tt, c), lambda i: (i, 1)),
            pl.BlockSpec((CONV_WIDTH, c), lambda i: (0, 0)),
            pl.BlockSpec((1, c), lambda i: (0, 0)),
            pl.BlockSpec((1, c), lambda i: (0, 0)),
            pl.BlockSpec((1, c), lambda i: (0, 0)),
        ],
        out_specs=pl.BlockSpec((tt, c), lambda i: (i, 0)),
        out_shape=jax.ShapeDtypeStruct((s, c), BF16),
        scratch_shapes=[pltpu.VMEM((SUBLANES, tt + CONV_HALO, c), F32), pltpu.VMEM((tt, c), F32),
                        pltpu.VMEM((CONV_WIDTH, SUBLANES, c), F32)],
        compiler_params=_params("arbitrary"),
        name="conv_module",
    )(proj, proj, conv_w, row(conv_b), row(ln_g), row(ln_b))


def _compress_kernel(kt_ref, pos_ref, w1_ref, w2_ref, o_ref):
    kt = kt_ref[0, 0].astype(F32)
    half = kt.shape[1]
    first = (kt + pos_ref[0, 0:1, :]).astype(BF16)
    second = (kt + pos_ref[0, 1:2, :]).astype(BF16)
    p = jnp.dot(first, w1_ref[0, 0:half, :], preferred_element_type=F32)
    q = jnp.dot(second, w1_ref[0, half:, :], preferred_element_type=F32)
    n = q.shape[0]
    h = p + pltpu.roll(q, n - 1, axis=0)
    h = h * jax.nn.sigmoid(h)
    o_ref[0, 0] = jnp.dot(h.astype(BF16), w2_ref[0], preferred_element_type=F32).astype(o_ref.dtype)


def compress(kt, pos, w1, w2):
    two, g, nch, half = kt.shape
    hid = w1.shape[2]
    dh = w2.shape[2]
    return pl.pallas_call(
        _compress_kernel,
        grid=(two, g),
        in_specs=[
            pl.BlockSpec((1, 1, nch, half), lambda a, b: (a, b, 0, 0)),
            pl.BlockSpec((1, 2, half), lambda a, b: (a, 0, 0)),
            pl.BlockSpec((1, 2 * half, hid), lambda a, b: (a, 0, 0)),
            pl.BlockSpec((1, hid, dh), lambda a, b: (a, 0, 0)),
        ],
        out_specs=pl.BlockSpec((1, 1, nch, dh), lambda a, b: (a, b, 0, 0)),
        out_shape=jax.ShapeDtypeStruct((two, g, nch, dh), BF16),
        compiler_params=_params("parallel", "parallel"),
        name="compress",
    )(kt, pos, w1, w2)


def _split3(x):
    hi = x.astype(BF16)
    r1 = x - hi.astype(F32)
    mid = r1.astype(BF16)
    lo = (r1 - mid.astype(F32)).astype(BF16)
    return hi, mid, lo


def _nsa_kernel(qt_ref, kc_ref, vct_ref, oh_ref, ks_ref, vst_ref, kw_ref, vwt_ref, glt_ref, o_ref,
                qa_ref, m_ref, acc_ref, s_ref, part_ref, *, seq):
    i = pl.program_id(1)
    s0 = i * Q_BLOCK
    dh = NSA_HEAD_DIM
    cols = NSA_GROUP * Q_BLOCK
    pair = 2 * Q_BLOCK
    n_sel = seq // SEL_LEN
    n_cmp = kc_ref.shape[1]

    def with_ones(vt):
        return jnp.concatenate([vt, jnp.ones((ONES_ROWS, vt.shape[1]), BF16)], axis=0)

    qt = jnp.concatenate([qt_ref[0, r * dh:(r + 1) * dh, :] for r in range(NSA_GROUP)], axis=1)
    qt = (qt.astype(F32) * (dh ** -0.5 * LOG2E)).astype(BF16)
    t_col = s0 + (lax.broadcasted_iota(jnp.int32, (1, cols), 1) & (Q_BLOCK - 1))

    sc = jnp.dot(kc_ref[0], qt, preferred_element_type=F32)
    cmp_end = lax.broadcasted_iota(jnp.int32, (n_cmp, 1), 0) * CMP_STRIDE + (CMP_LEN - 1)
    sc = jnp.where(cmp_end <= t_col, sc, NEG_INF)
    e = jnp.exp2(sc - jnp.max(sc, axis=0, keepdims=True))
    l = jnp.sum(e, axis=0, keepdims=True)
    pc = e * jnp.where(t_col >= CMP_LEN - 1, 1.0 / l, 0.0)
    oc = jnp.dot(vct_ref[0], pc.astype(BF16), preferred_element_type=F32)

    imp_c = pc[:, 0:Q_BLOCK]
    for r in range(1, NSA_GROUP):
        imp_c = imp_c + pc[:, r * Q_BLOCK:(r + 1) * Q_BLOCK]
    sj = lax.broadcasted_iota(jnp.int32, (n_sel, n_cmp), 0) * SEL_LEN
    cn = lax.broadcasted_iota(jnp.int32, (n_sel, n_cmp), 1) * CMP_STRIDE
    overlap = jnp.where((cn <= sj + SEL_LEN - 1) & (cn + CMP_LEN - 1 >= sj), 1.0, 0.0).astype(BF16)
    imp = jnp.zeros((n_sel, Q_BLOCK), F32)
    for part in _split3(imp_c):
        imp = imp + jnp.dot(overlap, part, preferred_element_type=F32)

    t_win = t_col + (jnp.min(imp, keepdims=True) < 0.0).astype(jnp.int32)
    wlen = WINDOW + Q_BLOCK
    w0 = pl.multiple_of(jnp.maximum(s0 - WINDOW, 0), Q_BLOCK)
    wc = w0 >> (T_CHUNK.bit_length() - 1)
    sw = jnp.dot(kw_ref[0, pl.ds(w0, wlen), :], qt, preferred_element_type=F32)
    wpos = w0 + lax.broadcasted_iota(jnp.int32, (wlen, 1), 0)
    sw = jnp.where(wpos <= t_win, sw, NEG_INF)
    sw = jnp.concatenate([jnp.where(wpos[0:Q_BLOCK] > t_win - WINDOW, sw[0:Q_BLOCK], NEG_INF), sw[Q_BLOCK:]],
                         axis=0)
    pw = jnp.exp2((sw - jnp.max(sw, axis=0, keepdims=True)).astype(BF16))
    accw = jnp.zeros((dh + ONES_ROWS, cols), F32)
    for j in range(wlen // T_CHUNK):
        accw = accw + jnp.dot(with_ones(vwt_ref[wc + j]), pw[j * T_CHUNK:(j + 1) * T_CHUNK],
                              preferred_element_type=F32)
    ow = accw[0:dh] * (1.0 / accw[dh:dh + 1])

    def gate_row(b):
        gate = jax.nn.sigmoid(glt_ref[0].astype(F32))
        return jnp.concatenate(
            [gate[N_BRANCH * r + b:N_BRANCH * r + b + 1] for r in range(NSA_GROUP)], axis=1)

    part_ref[...] = gate_row(0) * oc + gate_row(2) * ow


    jj = lax.broadcasted_iota(jnp.int32, (n_sel, Q_BLOCK), 0)
    cur = (s0 + lax.broadcasted_iota(jnp.int32, (1, Q_BLOCK), 1)) >> (SEL_LEN.bit_length() - 1)
    forced = (jj == 0) | (jj == cur) | (jj == cur - 1)
    future = jj > cur
    assert FORCE_BONUS > 2 * NSA_GROUP
    n_forced = 3
    work = jnp.where(forced, -jnp.inf, jnp.where(future, NEG_INF, imp))
    jf = jj.astype(F32)
    chosen = jnp.where(forced, 1.0, 0.0)
    for _ in range(min(SEL_TOPK, n_sel) - n_forced):
        mx = jnp.max(work, axis=0, keepdims=True)
        first = jnp.min(jnp.where(work == mx, jf, float(n_sel)), axis=0, keepdims=True)
        hit = jf == first
        chosen = jnp.where(hit, 1.0, chosen)
        work = jnp.where(hit, -jnp.inf, work)
    bias = jnp.where((chosen > 0.0) & jnp.logical_not(future), 0.0, NEG_INF).astype(BF16)
    qa_ref[0:n_sel, :] = jnp.concatenate([bias] * NSA_GROUP, axis=1)
    qa_ref[n_sel:, :] = qt

    m_ref[...] = jnp.full(m_ref.shape, NEG_INF, F32)
    acc_ref[...] = jnp.zeros(acc_ref.shape, F32)

    def scores(c, dst_ref):
        k0 = pl.multiple_of(c * SEL_CHUNK, SEL_CHUNK)
        ka = jnp.concatenate([oh_ref[pl.ds(k0, SEL_CHUNK), :], ks_ref[0, pl.ds(k0, SEL_CHUNK), :]],
                             axis=1)
        for h0 in range(0, cols, pair):
            dst_ref[:, h0:h0 + pair] = jnp.dot(ka, qa_ref[:, h0:h0 + pair], preferred_element_type=F32)

    def accumulate(c, src_ref, causal):
        s = src_ref[...]
        if causal:
            kpos = c * SEL_CHUNK + lax.broadcasted_iota(jnp.int32, (SEL_CHUNK, 1), 0)
            s = jnp.where(kpos <= t_col, s, NEG_INF)
        m_old = m_ref[...]
        m_new = jnp.maximum(m_old, jnp.max(s, axis=0, keepdims=True))
        p = jnp.exp2(s - m_new).astype(BF16)
        c0 = c * (SEL_CHUNK // T_CHUNK)
        vt = jnp.concatenate([vst_ref[c0 + k] for k in range(SEL_CHUNK // T_CHUNK)], axis=1)
        pv = jnp.dot(with_ones(vt), p, preferred_element_type=F32)
        acc_ref[...] = acc_ref[...] * jnp.exp2(m_old - m_new) + pv
        m_ref[...] = m_new

    n_full = s0 >> (SEL_CHUNK.bit_length() - 1)
    sa_ref, sb_ref = s_ref.at[0], s_ref.at[1]
    scores(0, sa_ref)

    def body(j, carry):
        c = 2 * j
        scores(c + 1, sb_ref)
        accumulate(c, sa_ref, False)
        scores(c + 2, sa_ref)
        accumulate(c + 1, sb_ref, False)
        return carry

    lax.fori_loop(0, n_full >> 1, body, 0)

    @pl.when((n_full & 1) == 1)
    def _():
        scores(n_full, sb_ref)
        accumulate(n_full - 1, sa_ref, False)

    accumulate(n_full, s_ref.at[n_full & 1], True)
    acc = acc_ref[...]
    os_ = acc[0:dh] * (1.0 / acc[dh:dh + 1])

    out_t = part_ref[...] + gate_row(1) * os_
    out_t = jnp.concatenate([out_t[:, r * Q_BLOCK:(r + 1) * Q_BLOCK] for r in range(NSA_GROUP)], axis=0)
    o_ref[...] = out_t.T.astype(o_ref.dtype)


def nsa_attention(proj_t, rows, kc, vct, onehot, ks, kw):
    nt = proj_t.shape[0]
    s = nt * T_CHUNK
    g, _, dh = kc.shape
    q_row, vs_row, vw_row, gate_row = rows
    assert Q_BLOCK == T_CHUNK and all(r % (NSA_GROUP * dh) == 0 for r in (q_row, vs_row, vw_row))
    assert gate_row % GATE_SLOT == 0
    dv = dh + ONES_ROWS
    cols = NSA_GROUP * Q_BLOCK
    whole = lambda a: pl.BlockSpec((1,) + a.shape[1:], lambda gi, i: (gi,) + (0,) * (a.ndim - 1))
    v_t = lambda row: pl.BlockSpec((nt, dh, T_CHUNK), lambda gi, i: (0, row // dh + gi, 0))
    return pl.pallas_call(
        functools.partial(_nsa_kernel, seq=s),
        grid=(g, s // Q_BLOCK),
        in_specs=[
            pl.BlockSpec((1, NSA_GROUP * dh, Q_BLOCK), lambda gi, i: (i, q_row // (NSA_GROUP * dh) + gi, 0)),
            whole(kc), whole(vct), pl.BlockSpec(onehot.shape, lambda gi, i: (0, 0)),
            whole(ks), v_t(vs_row), whole(kw), v_t(vw_row),
            pl.BlockSpec((1, GATE_SLOT, Q_BLOCK), lambda gi, i: (i, gate_row // GATE_SLOT + gi, 0)),
        ],
        out_specs=pl.BlockSpec((Q_BLOCK, NSA_GROUP * dh), lambda gi, i: (i, gi)),
        out_shape=jax.ShapeDtypeStruct((s, g * NSA_GROUP * dh), BF16),
        scratch_shapes=[pltpu.VMEM((onehot.shape[1] + dh, cols), BF16),
                        pltpu.VMEM((1, cols), F32),
                        pltpu.VMEM((dv, cols), F32),
                        pltpu.VMEM((2, SEL_CHUNK, cols), F32),
                        pltpu.VMEM((dh, cols), F32)],
        compiler_params=_params("parallel", "arbitrary"),
        name="nsa_attention",
    )(proj_t, kc, vct, onehot, ks, proj_t, kw, proj_t, proj_t)


def _out_proj_kernel(c_ref, o_ref, w_ref, x_ref, y_ref):
    half = c_ref.shape[1]
    acc = jnp.dot(c_ref[...], w_ref[0:half, :], preferred_element_type=F32)
    acc = acc + jnp.dot(o_ref[...], w_ref[half:, :], preferred_element_type=F32)
    y_ref[...] = x_ref[...] + acc


def out_proj(c, o, w, x, *, tm=1024, tn=512):
    m, half = c.shape
    n = w.shape[1]
    return pl.pallas_call(
        _out_proj_kernel,
        grid=(m // tm, n // tn),
        in_specs=[
            pl.BlockSpec((tm, half), lambda i, j: (i, 0)),
            pl.BlockSpec((tm, half), lambda i, j: (i, 0)),
            pl.BlockSpec((2 * half, tn), lambda i, j: (0, j)),
            pl.BlockSpec((tm, tn), lambda i, j: (i, j)),
        ],
        out_specs=pl.BlockSpec((tm, tn), lambda i, j: (i, j)),
        out_shape=jax.ShapeDtypeStruct((m, n), F32),
        compiler_params=_params("parallel", "parallel"),
        name="out_proj",
    )(c, o, w, x)


def _pool_kernel(x_ref, halo_ref, g_ref, w_ref, sc_ref, o_ref, *, tm):
    i = pl.program_id(0)
    x = x_ref[...]
    xn = _rms(x, g_ref[...])
    hn = _rms(halo_ref[...], g_ref[...]) * jnp.where(i > 0, 1.0, 0.0)
    ext = jnp.concatenate([hn, xn], axis=0)
    pd = w_ref.shape[1]
    pos1 = i * tm + 1 + lax.broadcasted_iota(jnp.int32, (tm, 1), 0)
    for gi, win in enumerate(POOL_WINDOWS):
        sl = slice(gi * pd, (gi + 1) * pd)
        s = ext[:, sl]
        sh = 1
        while sh < win:
            s = s + pltpu.roll(s, sh, axis=0)
            sh *= 2
        cnt = jnp.minimum(pos1, win).astype(F32)
        p = s[POOL_HALO:] / cnt - xn[:, sl]
        y = jnp.dot(p.astype(BF16), w_ref[gi], preferred_element_type=F32)
        o_ref[:, sl] = x[:, sl] + y * sc_ref[:, sl]


def pool_mixer(x, g, pool_w, pool_scale, *, tm=512):
    m, d = x.shape
    ng, pd, _ = pool_w.shape
    hb = tm // POOL_HALO
    return pl.pallas_call(
        functools.partial(_pool_kernel, tm=tm),
        grid=(m // tm,),
        in_specs=[
            pl.BlockSpec((tm, d), lambda i: (i, 0)),
            pl.BlockSpec((POOL_HALO, d), lambda i: (jnp.maximum(i * hb - 1, 0), 0)),
            pl.BlockSpec((1, d), lambda i: (0, 0)),
            pl.BlockSpec((ng, pd, pd), lambda i: (0, 0, 0)),
            pl.BlockSpec((1, d), lambda i: (0, 0)),
        ],
        out_specs=pl.BlockSpec((tm, d), lambda i: (i, 0)),
        out_shape=jax.ShapeDtypeStruct((m, d), F32),
        compiler_params=_params("parallel"),
        name="pool_mixer",
    )(x, x, g.reshape(1, d), pool_w, pool_scale.reshape(1, d))


def _even_mixer(x, g, w_in, w_out, conv_w, conv_b, ln_g, ln_b, pos_k, pos_v, kw1, kw2, vw1, vw2):
    s, d = x.shape
    dh, gk, hq = NSA_HEAD_DIM, NSA_KV_HEADS, NSA_HEADS
    conv_cols = 2 * conv_w.shape[1]
    q_cols, kv_cols = hq * dh, gk * dh
    tn = 512
    q0 = conv_cols
    kv = lambda k: w_in[:, q0 + q_cols + k * kv_cols:q0 + q_cols + (k + 1) * kv_cols]
    w_gate = w_in[:, q0 + q_cols + 6 * kv_cols:].reshape(d, gk, NSA_GROUP * N_BRANCH)
    w_gate = jnp.pad(w_gate, ((0, 0), (0, 0), (0, GATE_SLOT - NSA_GROUP * N_BRANCH))).reshape(d, gk * GATE_SLOT)
    nat = [w_in[:, :q0], kv(0), kv(1), kv(2), kv(4)]
    feat = [w_in[:, q0:q0 + q_cols], kv(3), kv(5), w_gate]
    n_nat_cols = sum(w.shape[1] for w in nat)
    n_feat_cols = sum(w.shape[1] for w in feat)
    assert n_nat_cols % tn == 0
    w_all = jnp.concatenate(nat + feat + [jnp.zeros((d, (-n_feat_cols) % tn), F32)], axis=1).astype(BF16)

    proj, proj_t = in_projection(x, g, w_all, n_nat_cols // tn, tn=tn)
    c = conv_module(proj, conv_w, conv_b, ln_g, ln_b)

    kv0 = conv_cols
    hm = lambda k: proj[:, kv0 + k * kv_cols:kv0 + (k + 1) * kv_cols].reshape(s, gk, dh).transpose(1, 0, 2)
    k_s, k_w = hm(2), hm(3)
    nch = s // CMP_STRIDE
    kt = (proj[:, kv0:kv0 + 2 * kv_cols].reshape(nch, CMP_STRIDE, 2, gk, dh)
          .transpose(2, 3, 0, 1, 4).reshape(2, gk, nch, CMP_STRIDE * dh))
    pos = jnp.stack([pos_k, pos_v]).reshape(2, 2, CMP_STRIDE * dh)
    kvc = compress(kt, pos, jnp.stack([kw1, vw1]).astype(BF16), jnp.stack([kw2, vw2]).astype(BF16))

    n_sel = s // SEL_LEN
    onehot = (jnp.arange(s)[:, None] // SEL_LEN == jnp.arange(n_sel)[None, :]).astype(BF16)
    rows = (0, q_cols, q_cols + kv_cols, q_cols + 2 * kv_cols)
    o = nsa_attention(proj_t, rows, kvc[0], kvc[1].transpose(0, 2, 1), onehot, k_s, k_w)
    return out_proj(c, o, w_out.astype(BF16), x)


def kernel(x, mix_norm, mlp_norm, w_mlp_in, w_mlp_out, w_in, w_out, conv_w, conv_b, conv_ln_g,
           conv_ln_b, cmp_pos_k, cmp_pos_v, cmp_k_w1, cmp_k_w2, cmp_v_w1, cmp_v_w2, pool_w,
           pool_scale, final_norm):
    b, s, d = x.shape
    depth = mix_norm.shape[0]
    outs = []
    for bi in range(b):
        h = x[bi]
        for layer in range(depth):
            i = layer // 2
            if layer % 2 == 0:
                h = _even_mixer(h, mix_norm[layer], w_in[i], w_out[i], conv_w[i], conv_b[i],
                                conv_ln_g[i], conv_ln_b[i], cmp_pos_k[i], cmp_pos_v[i],
                                cmp_k_w1[i], cmp_k_w2[i], cmp_v_w1[i], cmp_v_w2[i])
            else:
                h = pool_mixer(h, mix_norm[layer], pool_w[i].astype(BF16), pool_scale[i])
            fg = final_norm if layer == depth - 1 else None
            h = fused_mlp(h, mlp_norm[layer], w_mlp_in, w_mlp_out, layer, fg)
        outs.append(h)
    return jnp.stack(outs)
```

```python
import functools

import jax
import jax.numpy as jnp
from jax import lax
from jax.experimental import pallas as pl
from jax.experimental.pallas import tpu as pltpu

F32 = jnp.float32
BF16 = jnp.bfloat16

EPS = 1e-6
NEG_INF = -1e30
LOG2E = 1.4426950408889634
CONV_WIDTH = 31
SUBLANES = 8
CONV_HALO = 32
NSA_HEADS = 16
NSA_HEAD_DIM = 64
NSA_KV_HEADS = 4
NSA_GROUP = NSA_HEADS // NSA_KV_HEADS
N_BRANCH = 3
CMP_LEN = 32
CMP_STRIDE = 16
SEL_LEN = 64
SEL_TOPK = 16
WINDOW = 512
Q_BLOCK = 256
FORCE_BONUS = 1e3
POOL_WINDOWS = (2, 4, 8, 16)
POOL_HALO = 16
SEL_CHUNK = 512
ONES_ROWS = 16
T_CHUNK = 256
GATE_SLOT = 16

VMEM_LIMIT = 56 * 1024 * 1024


def _params(*sem):
    return pltpu.CompilerParams(dimension_semantics=sem, vmem_limit_bytes=VMEM_LIMIT)


def _rms(x, g):
    r = lax.rsqrt(jnp.mean(x * x, axis=-1, keepdims=True) + EPS)
    return x * r * g


def _in_proj_kernel(x_ref, g_ref, w_ref, on_ref, ot_ref, xn_ref, *, n_nat):
    j = pl.program_id(1)

    @pl.when(j == 0)
    def _():
        xn_ref[...] = _rms(x_ref[...], g_ref[...]).astype(BF16)

    acc = jnp.dot(xn_ref[...], w_ref[...], preferred_element_type=F32)

    @pl.when(j < n_nat)
    def _():
        on_ref[...] = acc.astype(on_ref.dtype)

    @pl.when(j >= n_nat)
    def _():
        acc_t = acc.T
        for k in range(ot_ref.shape[0]):
            ot_ref[k] = acc_t[:, k * T_CHUNK:(k + 1) * T_CHUNK].astype(ot_ref.dtype)


def in_projection(x, g, w, n_nat, *, tm=1024, tn=512):
    m, k = x.shape
    n_t = w.shape[1] // tn - n_nat
    assert w.shape[1] % tn == 0 and m % tm == 0 and tm % T_CHUNK == 0
    return pl.pallas_call(
        functools.partial(_in_proj_kernel, n_nat=n_nat),
        grid=(m // tm, n_nat + n_t),
        in_specs=[
            pl.BlockSpec((tm, k), lambda i, j: (i, 0)),
            pl.BlockSpec((1, k), lambda i, j: (0, 0)),
            pl.BlockSpec((k, tn), lambda i, j: (0, j)),
        ],
        out_specs=[
            pl.BlockSpec((tm, tn), lambda i, j: (i, jnp.minimum(j, n_nat - 1))),
            pl.BlockSpec((tm // T_CHUNK, tn, T_CHUNK), lambda i, j: (i, jnp.maximum(j - n_nat, 0), 0)),
        ],
        out_shape=[jax.ShapeDtypeStruct((m, n_nat * tn), BF16),
                   jax.ShapeDtypeStruct((m // T_CHUNK, n_t * tn, T_CHUNK), BF16)],
        scratch_shapes=[pltpu.VMEM((tm, k), BF16)],
        compiler_params=_params("parallel", "arbitrary"),
        name="in_projection",
    )(x, g.reshape(1, k), w)


def _mlp_kernel(x_ref, g_ref, w1_ref, w2_ref, gf_ref, o_ref, xn_ref, h_ref, *, final_norm):
    f = pl.program_id(1)

    @pl.when(f == 0)
    def _():
        x = x_ref[...]
        xn_ref[...] = _rms(x, g_ref[...]).astype(BF16)
        o_ref[...] = x
        h_ref[...] = jnp.zeros(h_ref.shape, BF16)

    h_new = jnp.dot(xn_ref[...], w1_ref[...].astype(BF16), preferred_element_type=F32)
    o_ref[...] += jnp.dot(h_ref[...], w2_ref[...].astype(BF16), preferred_element_type=F32)
    h_ref[...] = jnp.square(jnp.maximum(h_new, 0.0)).astype(BF16)

    if final_norm:
        @pl.when(f == pl.num_programs(1) - 1)
        def _():
            o_ref[...] = _rms(o_ref[...], gf_ref[...])


def fused_mlp(x, g, w1, w2, layer, gf=None, *, tm=2048, tf=256):
    m, d = x.shape
    nf = w1.shape[2] // tf
    final_norm = gf is not None
    if gf is None:
        gf = g
    once = pl.Buffered(1)
    return pl.pallas_call(
        functools.partial(_mlp_kernel, final_norm=final_norm),
        grid=(m // tm, nf + 1),
        in_specs=[
            pl.BlockSpec((tm, d), lambda i, f: (i, 0), pipeline_mode=once),
            pl.BlockSpec((1, d), lambda i, f: (0, 0)),
            pl.BlockSpec((None, d, tf), lambda i, f: (layer, 0, jnp.minimum(f, nf - 1))),
            pl.BlockSpec((None, tf, d), lambda i, f: (layer, jnp.maximum(f - 1, 0), 0)),
            pl.BlockSpec((1, d), lambda i, f: (0, 0)),
        ],
        out_specs=pl.BlockSpec((tm, d), lambda i, f: (i, 0), pipeline_mode=once),
        out_shape=jax.ShapeDtypeStruct((m, d), F32),
        scratch_shapes=[pltpu.VMEM((tm, d), BF16), pltpu.VMEM((tm, tf), BF16)],
        compiler_params=_params("parallel", "arbitrary"),
        name="fused_mlp",
    )(x, g.reshape(1, d), w1, w2, gf.reshape(1, d))


def _conv_kernel(a_ref, gt_ref, w_ref, b_ref, lg_ref, lb_ref, o_ref, cs_ref, y_ref, wb_ref, *, tt):
    i = pl.program_id(0)

    @pl.when(i == 0)
    def _():
        cs_ref[0, 0:CONV_HALO, :] = jnp.zeros((CONV_HALO, cs_ref.shape[2]), F32)
        for k in range(CONV_WIDTH):
            wb_ref[k] = jnp.broadcast_to(w_ref[k:k + 1, :], wb_ref.shape[1:])

    @pl.when(i > 0)
    def _():
        cs_ref[0, 0:CONV_HALO, :] = cs_ref[0, tt:tt + CONV_HALO, :]

    cs_ref[0, CONV_HALO:, :] = a_ref[...].astype(F32) * jax.nn.sigmoid(gt_ref[...].astype(F32))
    c_ext = cs_ref[0]
    for b in range(1, SUBLANES):
        cs_ref[b] = pltpu.roll(c_ext, b, axis=0)

    rows = 4 * SUBLANES

    def chunk(j, carry):
        r0 = pl.multiple_of(j * rows, rows)
        groups = range(rows // SUBLANES)
        acc = [jnp.zeros((SUBLANES, cs_ref.shape[2]), F32) + b_ref[...] for _ in groups]
        for k in range(CONV_WIDTH):
            a, b = divmod(k, SUBLANES)
            wk = wb_ref[CONV_WIDTH - 1 - k]
            for q in groups:
                lo = pl.multiple_of(r0 + CONV_HALO + (q - a) * SUBLANES, SUBLANES)
                acc[q] = acc[q] + cs_ref[b, pl.ds(lo, SUBLANES), :] * wk
        for q in groups:
            y_ref[pl.ds(pl.multiple_of(r0 + q * SUBLANES, SUBLANES), SUBLANES), :] = acc[q]
        return carry

    lax.fori_loop(0, tt // rows, chunk, 0)

    acc = y_ref[...]
    mu = jnp.mean(acc, axis=-1, keepdims=True)
    xc = acc - mu
    var = jnp.mean(xc * xc, axis=-1, keepdims=True)
    y = xc * lax.rsqrt(var + EPS) * lg_ref[...] + lb_ref[...]
    o_ref[...] = (y * jax.nn.sigmoid(y)).astype(o_ref.dtype)


def conv_module(proj, conv_w, conv_b, ln_g, ln_b, *, tt=512):
    s = proj.shape[0]
    c = conv_w.shape[1]
    row = lambda v: v.reshape(1, c)
    return pl.pallas_call(
        functools.partial(_conv_kernel, tt=tt),
        grid=(s // tt,),
        in_specs=[
            pl.BlockSpec((tt, c), lambda i: (i, 0)),
            pl.BlockSpec((tt, c), lambda i: (i, 1)),
            pl.BlockSpec((CONV_WIDTH, c), lambda i: (0, 0)),
            pl.BlockSpec((1, c), lambda i: (0, 0)),
            pl.BlockSpec((1, c), lambda i: (0, 0)),
            pl.BlockSpec((1, c), lambda i: (0, 0)),
        ],
        out_specs=pl.BlockSpec((tt, c), lambda i: (i, 0)),
        out_shape=jax.ShapeDtypeStruct((s, c), BF16),
        scratch_shapes=[pltpu.VMEM((SUBLANES, tt + CONV_HALO, c), F32), pltpu.VMEM((tt, c), F32),
                        pltpu.VMEM((CONV_WIDTH, SUBLANES, c), F32)],
        compiler_params=_params("arbitrary"),
        name="conv_module",
    )(proj, proj, conv_w, row(conv_b), row(ln_g), row(ln_b))


def _compress_kernel(kt_ref, pos_ref, w1_ref, w2_ref, o_ref):
    kt = kt_ref[0, 0].astype(F32)
    half = kt.shape[1]
    first = (kt + pos_ref[0, 0:1, :]).astype(BF16)
    second = (kt + pos_ref[0, 1:2, :]).astype(BF16)
    p = jnp.dot(first, w1_ref[0, 0:half, :], preferred_element_type=F32)
    q = jnp.dot(second, w1_ref[0, half:, :], preferred_element_type=F32)
    n = q.shape[0]
    h = p + pltpu.roll(q, n - 1, axis=0)
    h = h * jax.nn.sigmoid(h)
    o_ref[0, 0] = jnp.dot(h.astype(BF16), w2_ref[0], preferred_element_type=F32).astype(o_ref.dtype)


def compress(kt, pos, w1, w2):
    two, g, nch, half = kt.shape
    hid = w1.shape[2]
    dh = w2.shape[2]
    return pl.pallas_call(
        _compress_kernel,
        grid=(two, g),
        in_specs=[
            pl.BlockSpec((1, 1, nch, half), lambda a, b: (a, b, 0, 0)),
            pl.BlockSpec((1, 2, half), lambda a, b: (a, 0, 0)),
            pl.BlockSpec((1, 2 * half, hid), lambda a, b: (a, 0, 0)),
            pl.BlockSpec((1, hid, dh), lambda a, b: (a, 0, 0)),
        ],
        out_specs=pl.BlockSpec((1, 1, nch, dh), lambda a, b: (a, b, 0, 0)),
        out_shape=jax.ShapeDtypeStruct((two, g, nch, dh), BF16),
        compiler_params=_params("parallel", "parallel"),
        name="compress",
    )(kt, pos, w1, w2)


def _split3(x):
    hi = x.astype(BF16)
    r1 = x - hi.astype(F32)
    mid = r1.astype(BF16)
    lo = (r1 - mid.astype(F32)).astype(BF16)
    return hi, mid, lo


def _nsa_kernel(qt_ref, kc_ref, vct_ref, oh_ref, ks_ref, vst_ref, kw_ref, vwt_ref, glt_ref, o_ref,
                qa_ref, m_ref, acc_ref, s_ref, part_ref, *, seq):
    i = pl.program_id(1)
    s0 = i * Q_BLOCK
    dh = NSA_HEAD_DIM
    cols = NSA_GROUP * Q_BLOCK
    pair = 2 * Q_BLOCK
    n_sel = seq // SEL_LEN
    n_cmp = kc_ref.shape[1]

    def with_ones(vt):
        return jnp.concatenate([vt, jnp.ones((ONES_ROWS, vt.shape[1]), BF16)], axis=0)

    qt = jnp.concatenate([qt_ref[0, r * dh:(r + 1) * dh, :] for r in range(NSA_GROUP)], axis=1)
    qt = (qt.astype(F32) * (dh ** -0.5 * LOG2E)).astype(BF16)
    t_col = s0 + (lax.broadcasted_iota(jnp.int32, (1, cols), 1) & (Q_BLOCK - 1))

    sc = jnp.dot(kc_ref[0], qt, preferred_element_type=F32)
    cmp_end = lax.broadcasted_iota(jnp.int32, (n_cmp, 1), 0) * CMP_STRIDE + (CMP_LEN - 1)
    sc = jnp.where(cmp_end <= t_col, sc, NEG_INF)
    e = jnp.exp2(sc - jnp.max(sc, axis=0, keepdims=True))
    l = jnp.sum(e, axis=0, keepdims=True)
    pc = e * jnp.where(t_col >= CMP_LEN - 1, 1.0 / l, 0.0)
    oc = jnp.dot(vct_ref[0], pc.astype(BF16), preferred_element_type=F32)

    imp_c = pc[:, 0:Q_BLOCK]
    for r in range(1, NSA_GROUP):
        imp_c = imp_c + pc[:, r * Q_BLOCK:(r + 1) * Q_BLOCK]
    sj = lax.broadcasted_iota(jnp.int32, (n_sel, n_cmp), 0) * SEL_LEN
    cn = lax.broadcasted_iota(jnp.int32, (n_sel, n_cmp), 1) * CMP_STRIDE
    overlap = jnp.where((cn <= sj + SEL_LEN - 1) & (cn + CMP_LEN - 1 >= sj), 1.0, 0.0).astype(BF16)
    imp = jnp.zeros((n_sel, Q_BLOCK), F32)
    for part in _split3(imp_c):
        imp = imp + jnp.dot(overlap, part, preferred_element_type=F32)

    t_win = t_col + (jnp.min(imp, keepdims=True) < 0.0).astype(jnp.int32)
    wlen = WINDOW + Q_BLOCK
    w0 = pl.multiple_of(jnp.maximum(s0 - WINDOW, 0), Q_BLOCK)
    wc = w0 >> (T_CHUNK.bit_length() - 1)
    sw = jnp.dot(kw_ref[0, pl.ds(w0, wlen), :], qt, preferred_element_type=F32)
    wpos = w0 + lax.broadcasted_iota(jnp.int32, (wlen, 1), 0)
    sw = jnp.where(wpos <= t_win, sw, NEG_INF)
    sw = jnp.concatenate([jnp.where(wpos[0:Q_BLOCK] > t_win - WINDOW, sw[0:Q_BLOCK], NEG_INF), sw[Q_BLOCK:]],
                         axis=0)
    pw = jnp.exp2((sw - jnp.max(sw, axis=0, keepdims=True)).astype(BF16))
    accw = jnp.zeros((dh + ONES_ROWS, cols), F32)
    for j in range(wlen // T_CHUNK):
        accw = accw + jnp.dot(with_ones(vwt_ref[wc + j]), pw[j * T_CHUNK:(j + 1) * T_CHUNK],
                              preferred_element_type=F32)
    ow = accw[0:dh] * (1.0 / accw[dh:dh + 1])

    def gate_row(b):
        gate = jax.nn.sigmoid(glt_ref[0].astype(F32))
        return jnp.concatenate(
            [gate[N_BRANCH * r + b:N_BRANCH * r + b + 1] for r in range(NSA_GROUP)], axis=1)

    part_ref[...] = gate_row(0) * oc + gate_row(2) * ow


    jj = lax.broadcasted_iota(jnp.int32, (n_sel, Q_BLOCK), 0)
    cur = (s0 + lax.broadcasted_iota(jnp.int32, (1, Q_BLOCK), 1)) >> (SEL_LEN.bit_length() - 1)
    forced = (jj == 0) | (jj == cur) | (jj == cur - 1)
    future = jj > cur
    assert FORCE_BONUS > 2 * NSA_GROUP
    n_forced = 3
    work = jnp.where(forced, -jnp.inf, jnp.where(future, NEG_INF, imp))
    jf = jj.astype(F32)
    chosen = jnp.where(forced, 1.0, 0.0)
    for _ in range(min(SEL_TOPK, n_sel) - n_forced):
        mx = jnp.max(work, axis=0, keepdims=True)
        first = jnp.min(jnp.where(work == mx, jf, float(n_sel)), axis=0, keepdims=True)
        hit = jf == first
        chosen = jnp.where(hit, 1.0, chosen)
        work = jnp.where(hit, -jnp.inf, work)
    bias = jnp.where((chosen > 0.0) & jnp.logical_not(future), 0.0, NEG_INF).astype(BF16)
    qa_ref[0:n_sel, :] = jnp.concatenate([bias] * NSA_GROUP, axis=1)
    qa_ref[n_sel:, :] = qt

    m_ref[...] = jnp.full(m_ref.shape, NEG_INF, F32)
    acc_ref[...] = jnp.zeros(acc_ref.shape, F32)

    def scores(c, dst_ref):
        k0 = pl.multiple_of(c * SEL_CHUNK, SEL_CHUNK)
        ka = jnp.concatenate([oh_ref[pl.ds(k0, SEL_CHUNK), :], ks_ref[0, pl.ds(k0, SEL_CHUNK), :]],
                             axis=1)
        for h0 in range(0, cols, pair):
            dst_ref[:, h0:h0 + pair] = jnp.dot(ka, qa_ref[:, h0:h0 + pair], preferred_element_type=F32)

    def accumulate(c, src_ref, causal, n_keys=SEL_CHUNK):
        s = src_ref[0:n_keys, :]
        if causal:
            kpos = c * SEL_CHUNK + lax.broadcasted_iota(jnp.int32, (n_keys, 1), 0)
            s = jnp.where(kpos <= t_col, s, NEG_INF)
        m_old = m_ref[...]
        m_new = jnp.maximum(m_old, jnp.max(s, axis=0, keepdims=True))
        p = jnp.exp2(s - m_new).astype(BF16)
        c0 = c * (SEL_CHUNK // T_CHUNK)
        vt = jnp.concatenate([vst_ref[c0 + k] for k in range(n_keys // T_CHUNK)], axis=1)
        pv = jnp.dot(with_ones(vt), p, preferred_element_type=F32)
        acc_ref[...] = acc_ref[...] * jnp.exp2(m_old - m_new) + pv
        m_ref[...] = m_new

    n_full = s0 >> (SEL_CHUNK.bit_length() - 1)
    sa_ref, sb_ref = s_ref.at[0], s_ref.at[1]
    scores(0, sa_ref)

    def body(j, carry):
        c = 2 * j
        scores(c + 1, sb_ref)
        accumulate(c, sa_ref, False)
        scores(c + 2, sa_ref)
        accumulate(c + 1, sb_ref, False)
        return carry

    lax.fori_loop(0, n_full >> 1, body, 0)

    @pl.when((n_full & 1) == 1)
    def _():
        scores(n_full, sb_ref)
        accumulate(n_full - 1, sa_ref, False)

    own = s0 - n_full * SEL_CHUNK
    for k in range(SEL_CHUNK // Q_BLOCK):
        @pl.when(own == k * Q_BLOCK)
        def _(k=k):
            accumulate(n_full, s_ref.at[n_full & 1], True, (k + 1) * Q_BLOCK)

    acc = acc_ref[...]
    os_ = acc[0:dh] * (1.0 / acc[dh:dh + 1])

    out_t = part_ref[...] + gate_row(1) * os_
    out_t = jnp.concatenate([out_t[:, r * Q_BLOCK:(r + 1) * Q_BLOCK] for r in range(NSA_GROUP)], axis=0)
    o_ref[...] = out_t.T.astype(o_ref.dtype)


def nsa_attention(proj_t, rows, kc, vct, onehot, ks, kw):
    nt = proj_t.shape[0]
    s = nt * T_CHUNK
    g, _, dh = kc.shape
    q_row, vs_row, vw_row, gate_row = rows
    assert Q_BLOCK == T_CHUNK and all(r % (NSA_GROUP * dh) == 0 for r in (q_row, vs_row, vw_row))
    assert gate_row % GATE_SLOT == 0
    dv = dh + ONES_ROWS
    cols = NSA_GROUP * Q_BLOCK
    whole = lambda a: pl.BlockSpec((1,) + a.shape[1:], lambda gi, i: (gi,) + (0,) * (a.ndim - 1))
    v_t = lambda row: pl.BlockSpec((nt, dh, T_CHUNK), lambda gi, i: (0, row // dh + gi, 0))
    return pl.pallas_call(
        functools.partial(_nsa_kernel, seq=s),
        grid=(g, s // Q_BLOCK),
        in_specs=[
            pl.BlockSpec((1, NSA_GROUP * dh, Q_BLOCK), lambda gi, i: (i, q_row // (NSA_GROUP * dh) + gi, 0)),
            whole(kc), whole(vct), pl.BlockSpec(onehot.shape, lambda gi, i: (0, 0)),
            whole(ks), v_t(vs_row), whole(kw), v_t(vw_row),
            pl.BlockSpec((1, GATE_SLOT, Q_BLOCK), lambda gi, i: (i, gate_row // GATE_SLOT + gi, 0)),
        ],
        out_specs=pl.BlockSpec((Q_BLOCK, NSA_GROUP * dh), lambda gi, i: (i, gi)),
        out_shape=jax.ShapeDtypeStruct((s, g * NSA_GROUP * dh), BF16),
        scratch_shapes=[pltpu.VMEM((onehot.shape[1] + dh, cols), BF16),
                        pltpu.VMEM((1, cols), F32),
                        pltpu.VMEM((dv, cols), F32),
                        pltpu.VMEM((2, SEL_CHUNK, cols), F32),
                        pltpu.VMEM((dh, cols), F32)],
        compiler_params=_params("parallel", "arbitrary"),
        name="nsa_attention",
    )(proj_t, kc, vct, onehot, ks, proj_t, kw, proj_t, proj_t)


def _out_proj_kernel(c_ref, o_ref, w_ref, x_ref, y_ref):
    half = c_ref.shape[1]
    acc = jnp.dot(c_ref[...], w_ref[0:half, :], preferred_element_type=F32)
    acc = acc + jnp.dot(o_ref[...], w_ref[half:, :], preferred_element_type=F32)
    y_ref[...] = x_ref[...] + acc


def out_proj(c, o, w, x, *, tm=1024, tn=512):
    m, half = c.shape
    n = w.shape[1]
    return pl.pallas_call(
        _out_proj_kernel,
        grid=(m // tm, n // tn),
        in_specs=[
            pl.BlockSpec((tm, half), lambda i, j: (i, 0)),
            pl.BlockSpec((tm, half), lambda i, j: (i, 0)),
            pl.BlockSpec((2 * half, tn), lambda i, j: (0, j)),
            pl.BlockSpec((tm, tn), lambda i, j: (i, j)),
        ],
        out_specs=pl.BlockSpec((tm, tn), lambda i, j: (i, j)),
        out_shape=jax.ShapeDtypeStruct((m, n), F32),
        compiler_params=_params("parallel", "parallel"),
        name="out_proj",
    )(c, o, w, x)


def _pool_kernel(x_ref, halo_ref, g_ref, w_ref, sc_ref, o_ref, *, tm):
    i = pl.program_id(0)
    x = x_ref[...]
    xn = _rms(x, g_ref[...])
    hn = _rms(halo_ref[...], g_ref[...]) * jnp.where(i > 0, 1.0, 0.0)
    ext = jnp.concatenate([hn, xn], axis=0)
    pd = w_ref.shape[1]
    pos1 = i * tm + 1 + lax.broadcasted_iota(jnp.int32, (tm, 1), 0)
    for gi, win in enumerate(POOL_WINDOWS):
        sl = slice(gi * pd, (gi + 1) * pd)
        s = ext[:, sl]
        sh = 1
        while sh < win:
            s = s + pltpu.roll(s, sh, axis=0)
            sh *= 2
        cnt = jnp.minimum(pos1, win).astype(F32)
        p = s[POOL_HALO:] / cnt - xn[:, sl]
        y = jnp.dot(p.astype(BF16), w_ref[gi], preferred_element_type=F32)
        o_ref[:, sl] = x[:, sl] + y * sc_ref[:, sl]


def pool_mixer(x, g, pool_w, pool_scale, *, tm=512):
    m, d = x.shape
    ng, pd, _ = pool_w.shape
    hb = tm // POOL_HALO
    return pl.pallas_call(
        functools.partial(_pool_kernel, tm=tm),
        grid=(m // tm,),
        in_specs=[
            pl.BlockSpec((tm, d), lambda i: (i, 0)),
            pl.BlockSpec((POOL_HALO, d), lambda i: (jnp.maximum(i * hb - 1, 0), 0)),
            pl.BlockSpec((1, d), lambda i: (0, 0)),
            pl.BlockSpec((ng, pd, pd), lambda i: (0, 0, 0)),
            pl.BlockSpec((1, d), lambda i: (0, 0)),
        ],
        out_specs=pl.BlockSpec((tm, d), lambda i: (i, 0)),
        out_shape=jax.ShapeDtypeStruct((m, d), F32),
        compiler_params=_params("parallel"),
        name="pool_mixer",
    )(x, x, g.reshape(1, d), pool_w, pool_scale.reshape(1, d))


def _even_mixer(x, g, w_in, w_out, conv_w, conv_b, ln_g, ln_b, pos_k, pos_v, kw1, kw2, vw1, vw2):
    s, d = x.shape
    dh, gk, hq = NSA_HEAD_DIM, NSA_KV_HEADS, NSA_HEADS
    conv_cols = 2 * conv_w.shape[1]
    q_cols, kv_cols = hq * dh, gk * dh
    tn = 512
    q0 = conv_cols
    kv = lambda k: w_in[:, q0 + q_cols + k * kv_cols:q0 + q_cols + (k + 1) * kv_cols]
    w_gate = w_in[:, q0 + q_cols + 6 * kv_cols:].reshape(d, gk, NSA_GROUP * N_BRANCH)
    w_gate = jnp.pad(w_gate, ((0, 0), (0, 0), (0, GATE_SLOT - NSA_GROUP * N_BRANCH))).reshape(d, gk * GATE_SLOT)
    nat = [w_in[:, :q0], kv(0), kv(1), kv(2), kv(4)]
    feat = [w_in[:, q0:q0 + q_cols], kv(3), kv(5), w_gate]
    n_nat_cols = sum(w.shape[1] for w in nat)
    n_feat_cols = sum(w.shape[1] for w in feat)
    assert n_nat_cols % tn == 0
    w_all = jnp.concatenate(nat + feat + [jnp.zeros((d, (-n_feat_cols) % tn), F32)], axis=1).astype(BF16)

    proj, proj_t = in_projection(x, g, w_all, n_nat_cols // tn, tn=tn)
    c = conv_module(proj, conv_w, conv_b, ln_g, ln_b)

    kv0 = conv_cols
    hm = lambda k: proj[:, kv0 + k * kv_cols:kv0 + (k + 1) * kv_cols].reshape(s, gk, dh).transpose(1, 0, 2)
    k_s, k_w = hm(2), hm(3)
    nch = s // CMP_STRIDE
    kt = jnp.stack([hm(0), hm(1)]).reshape(2, gk, nch, CMP_STRIDE * dh)
    pos = jnp.stack([pos_k, pos_v]).reshape(2, 2, CMP_STRIDE * dh)
    kvc = compress(kt, pos, jnp.stack([kw1, vw1]).astype(BF16), jnp.stack([kw2, vw2]).astype(BF16))

    n_sel = s // SEL_LEN
    onehot = (jnp.arange(s)[:, None] // SEL_LEN == jnp.arange(n_sel)[None, :]).astype(BF16)
    rows = (0, q_cols, q_cols + kv_cols, q_cols + 2 * kv_cols)
    o = nsa_attention(proj_t, rows, kvc[0], kvc[1].transpose(0, 2, 1), onehot, k_s, k_w)
    return out_proj(c, o, w_out.astype(BF16), x)


def kernel(x, mix_norm, mlp_norm, w_mlp_in, w_mlp_out, w_in, w_out, conv_w, conv_b, conv_ln_g,
           conv_ln_b, cmp_pos_k, cmp_pos_v, cmp_k_w1, cmp_k_w2, cmp_v_w1, cmp_v_w2, pool_w,
           pool_scale, final_norm):
    b, s, d = x.shape
    depth = mix_norm.shape[0]
    outs = []
    for bi in range(b):
        h = x[bi]
        for layer in range(depth):
            i = layer // 2
            if layer % 2 == 0:
                h = _even_mixer(h, mix_norm[layer], w_in[i], w_out[i], conv_w[i], conv_b[i],
                                conv_ln_g[i], conv_ln_b[i], cmp_pos_k[i], cmp_pos_v[i],
                                cmp_k_w1[i], cmp_k_w2[i], cmp_v_w1[i], cmp_v_w2[i])
            else:
                h = pool_mixer(h, mix_norm[layer], pool_w[i].astype(BF16), pool_scale[i])
            fg = final_norm if layer == depth - 1 else None
            h = fused_mlp(h, mlp_norm[layer], w_mlp_in, w_mlp_out, layer, fg)
        outs.append(h)
    return jnp.stack(outs)
```

```python
import functools

import jax
import jax.numpy as jnp
from jax import lax
from jax.experimental import pallas as pl
from jax.experimental.pallas import tpu as pltpu

F32 = jnp.float32
BF16 = jnp.bfloat16

EPS = 1e-6
NEG_INF = -1e30
LOG2E = 1.4426950408889634
CONV_WIDTH = 31
SUBLANES = 8
CONV_HALO = 32
NSA_HEADS = 16
NSA_HEAD_DIM = 64
NSA_KV_HEADS = 4
NSA_GROUP = NSA_HEADS // NSA_KV_HEADS
N_BRANCH = 3
CMP_LEN = 32
CMP_STRIDE = 16
SEL_LEN = 64
SEL_TOPK = 16
WINDOW = 512
Q_BLOCK = 256
FORCE_BONUS = 1e3
POOL_WINDOWS = (2, 4, 8, 16)
POOL_HALO = 16
SEL_CHUNK = 512
ONES_ROWS = 16
T_CHUNK = 256
GATE_SLOT = 16

VMEM_LIMIT = 56 * 1024 * 1024


def _params(*sem):
    return pltpu.CompilerParams(dimension_semantics=sem, vmem_limit_bytes=VMEM_LIMIT)


def _rms(x, g):
    r = lax.rsqrt(jnp.mean(x * x, axis=-1, keepdims=True) + EPS)
    return x * r * g


def _in_proj_kernel(x_ref, g_ref, w_ref, on_ref, ot_ref, xn_ref, *, n_nat):
    j = pl.program_id(1)

    @pl.when(j == 0)
    def _():
        xn_ref[...] = _rms(x_ref[...], g_ref[...]).astype(BF16)

    acc = jnp.dot(xn_ref[...], w_ref[...], preferred_element_type=F32)

    @pl.when(j < n_nat)
    def _():
        on_ref[...] = acc.astype(on_ref.dtype)

    @pl.when(j >= n_nat)
    def _():
        acc_t = acc.T
        for k in range(ot_ref.shape[0]):
            ot_ref[k] = acc_t[:, k * T_CHUNK:(k + 1) * T_CHUNK].astype(ot_ref.dtype)


def in_projection(x, g, w, n_nat, *, tm=1024, tn=512):
    m, k = x.shape
    n_t = w.shape[1] // tn - n_nat
    assert w.shape[1] % tn == 0 and m % tm == 0 and tm % T_CHUNK == 0
    return pl.pallas_call(
        functools.partial(_in_proj_kernel, n_nat=n_nat),
        grid=(m // tm, n_nat + n_t),
        in_specs=[
            pl.BlockSpec((tm, k), lambda i, j: (i, 0)),
            pl.BlockSpec((1, k), lambda i, j: (0, 0)),
            pl.BlockSpec((k, tn), lambda i, j: (0, j)),
        ],
        out_specs=[
            pl.BlockSpec((tm, tn), lambda i, j: (i, jnp.minimum(j, n_nat - 1))),
            pl.BlockSpec((tm // T_CHUNK, tn, T_CHUNK), lambda i, j: (i, jnp.maximum(j - n_nat, 0), 0)),
        ],
        out_shape=[jax.ShapeDtypeStruct((m, n_nat * tn), BF16),
                   jax.ShapeDtypeStruct((m // T_CHUNK, n_t * tn, T_CHUNK), BF16)],
        scratch_shapes=[pltpu.VMEM((tm, k), BF16)],
        compiler_params=_params("parallel", "arbitrary"),
        name="in_projection",
    )(x, g.reshape(1, k), w)


def _mlp_kernel(x_ref, g_ref, w1_ref, w2_ref, gf_ref, o_ref, xn_ref, h_ref, *, final_norm):
    f = pl.program_id(1)

    @pl.when(f == 0)
    def _():
        x = x_ref[...]
        xn_ref[...] = _rms(x, g_ref[...]).astype(BF16)
        o_ref[...] = x
        h_ref[...] = jnp.zeros(h_ref.shape, BF16)

    h_new = jnp.dot(xn_ref[...], w1_ref[...].astype(BF16), preferred_element_type=F32)
    o_ref[...] += jnp.dot(h_ref[...], w2_ref[...].astype(BF16), preferred_element_type=F32)
    h_ref[...] = jnp.square(jnp.maximum(h_new, 0.0)).astype(BF16)

    if final_norm:
        @pl.when(f == pl.num_programs(1) - 1)
        def _():
            o_ref[...] = _rms(o_ref[...], gf_ref[...])


def fused_mlp(x, g, w1, w2, layer, gf=None, *, tm=2048, tf=256):
    m, d = x.shape
    nf = w1.shape[2] // tf
    final_norm = gf is not None
    if gf is None:
        gf = g
    once = pl.Buffered(1)
    return pl.pallas_call(
        functools.partial(_mlp_kernel, final_norm=final_norm),
        grid=(m // tm, nf + 1),
        in_specs=[
            pl.BlockSpec((tm, d), lambda i, f: (i, 0), pipeline_mode=once),
            pl.BlockSpec((1, d), lambda i, f: (0, 0)),
            pl.BlockSpec((None, d, tf), lambda i, f: (layer, 0, jnp.minimum(f, nf - 1))),
            pl.BlockSpec((None, tf, d), lambda i, f: (layer, jnp.maximum(f - 1, 0), 0)),
            pl.BlockSpec((1, d), lambda i, f: (0, 0)),
        ],
        out_specs=pl.BlockSpec((tm, d), lambda i, f: (i, 0), pipeline_mode=once),
        out_shape=jax.ShapeDtypeStruct((m, d), F32),
        scratch_shapes=[pltpu.VMEM((tm, d), BF16), pltpu.VMEM((tm, tf), BF16)],
        compiler_params=_params("parallel", "arbitrary"),
        name="fused_mlp",
    )(x, g.reshape(1, d), w1, w2, gf.reshape(1, d))


def _conv_kernel(a_ref, gt_ref, w_ref, b_ref, lg_ref, lb_ref, o_ref, cs_ref, y_ref, wb_ref, *, tt):
    i = pl.program_id(0)

    @pl.when(i == 0)
    def _():
        cs_ref[0, 0:CONV_HALO, :] = jnp.zeros((CONV_HALO, cs_ref.shape[2]), F32)
        for k in range(CONV_WIDTH):
            wb_ref[k] = jnp.broadcast_to(w_ref[k:k + 1, :], wb_ref.shape[1:])

    @pl.when(i > 0)
    def _():
        cs_ref[0, 0:CONV_HALO, :] = cs_ref[0, tt:tt + CONV_HALO, :]

    cs_ref[0, CONV_HALO:, :] = a_ref[...].astype(F32) * jax.nn.sigmoid(gt_ref[...].astype(F32))
    c_ext = cs_ref[0]
    for b in range(1, SUBLANES):
        cs_ref[b] = pltpu.roll(c_ext, b, axis=0)

    rows = 4 * SUBLANES

    def chunk(j, carry):
        r0 = pl.multiple_of(j * rows, rows)
        groups = range(rows // SUBLANES)
        acc = [jnp.zeros((SUBLANES, cs_ref.shape[2]), F32) + b_ref[...] for _ in groups]
        for k in range(CONV_WIDTH):
            a, b = divmod(k, SUBLANES)
            wk = wb_ref[CONV_WIDTH - 1 - k]
            for q in groups:
                lo = pl.multiple_of(r0 + CONV_HALO + (q - a) * SUBLANES, SUBLANES)
                acc[q] = acc[q] + cs_ref[b, pl.ds(lo, SUBLANES), :] * wk
        for q in groups:
            y_ref[pl.ds(pl.multiple_of(r0 + q * SUBLANES, SUBLANES), SUBLANES), :] = acc[q]
        return carry

    lax.fori_loop(0, tt // rows, chunk, 0)

    acc = y_ref[...]
    mu = jnp.mean(acc, axis=-1, keepdims=True)
    xc = acc - mu
    var = jnp.mean(xc * xc, axis=-1, keepdims=True)
    y = xc * lax.rsqrt(var + EPS) * lg_ref[...] + lb_ref[...]
    o_ref[...] = (y * jax.nn.sigmoid(y)).astype(o_ref.dtype)


def conv_module(proj, conv_w, conv_b, ln_g, ln_b, *, tt=512):
    s = proj.shape[0]
    c = conv_w.shape[1]
    row = lambda v: v.reshape(1, c)
    return pl.pallas_call(
        functools.partial(_conv_kernel, tt=tt),
        grid=(s // tt,),
        in_specs=[
            pl.BlockSpec((tt, c), lambda i: (i, 0)),
            pl.BlockSpec((tt, c), lambda i: (i, 1)),
            pl.BlockSpec((CONV_WIDTH, c), lambda i: (0, 0)),
            pl.BlockSpec((1, c), lambda i: (0, 0)),
            pl.BlockSpec((1, c), lambda i: (0, 0)),
            pl.BlockSpec((1, c), lambda i: (0, 0)),
        ],
        out_specs=pl.BlockSpec((tt, c), lambda i: (i, 0)),
        out_shape=jax.ShapeDtypeStruct((s, c), BF16),
        scratch_shapes=[pltpu.VMEM((SUBLANES, tt + CONV_HALO, c), F32), pltpu.VMEM((tt, c), F32),
                        pltpu.VMEM((CONV_WIDTH, SUBLANES, c), F32)],
        compiler_params=_params("arbitrary"),
        name="conv_module",
    )(proj, proj, conv_w, row(conv_b), row(ln_g), row(ln_b))


def _compress_kernel(kt_ref, pos_ref, w1_ref, w2_ref, o_ref):
    kt = kt_ref[0, 0].astype(F32)
    half = kt.shape[1]
    first = (kt + pos_ref[0, 0:1, :]).astype(BF16)
    second = (kt + pos_ref[0, 1:2, :]).astype(BF16)
    p = jnp.dot(first, w1_ref[0, 0:half, :], preferred_element_type=F32)
    q = jnp.dot(second, w1_ref[0, half:, :], preferred_element_type=F32)
    n = q.shape[0]
    h = p + pltpu.roll(q, n - 1, axis=0)
    h = h * jax.nn.sigmoid(h)
    o_ref[0, 0] = jnp.dot(h.astype(BF16), w2_ref[0], preferred_element_type=F32).astype(o_ref.dtype)


def compress(kt, pos, w1, w2):
    two, g, nch, half = kt.shape
    hid = w1.shape[2]
    dh = w2.shape[2]
    return pl.pallas_call(
        _compress_kernel,
        grid=(two, g),
        in_specs=[
            pl.BlockSpec((1, 1, nch, half), lambda a, b: (a, b, 0, 0)),
            pl.BlockSpec((1, 2, half), lambda a, b: (a, 0, 0)),
            pl.BlockSpec((1, 2 * half, hid), lambda a, b: (a, 0, 0)),
            pl.BlockSpec((1, hid, dh), lambda a, b: (a, 0, 0)),
        ],
        out_specs=pl.BlockSpec((1, 1, nch, dh), lambda a, b: (a, b, 0, 0)),
        out_shape=jax.ShapeDtypeStruct((two, g, nch, dh), BF16),
        compiler_params=_params("parallel", "parallel"),
        name="compress",
    )(kt, pos, w1, w2)


def _split3(x):
    hi = x.astype(BF16)
    r1 = x - hi.astype(F32)
    mid = r1.astype(BF16)
    lo = (r1 - mid.astype(F32)).astype(BF16)
    return hi, mid, lo


def _nsa_kernel(qt_ref, kc_ref, vct_ref, oh_ref, ks_ref, vst_ref, kw_ref, vwt_ref, glt_ref, o_ref,
                qa_ref, m_ref, acc_ref, s_ref, part_ref, *, seq):
    i = pl.program_id(1)
    s0 = i * Q_BLOCK
    dh = NSA_HEAD_DIM
    cols = NSA_GROUP * Q_BLOCK
    pair = 2 * Q_BLOCK
    n_sel = seq // SEL_LEN
    n_cmp = kc_ref.shape[1]

    def with_ones(vt):
        return jnp.concatenate([vt, jnp.ones((ONES_ROWS, vt.shape[1]), BF16)], axis=0)

    qt = jnp.concatenate([qt_ref[0, r * dh:(r + 1) * dh, :] for r in range(NSA_GROUP)], axis=1)
    qt = (qt.astype(F32) * (dh ** -0.5 * LOG2E)).astype(BF16)
    t_col = s0 + (lax.broadcasted_iota(jnp.int32, (1, cols), 1) & (Q_BLOCK - 1))

    sc = jnp.dot(kc_ref[0], qt, preferred_element_type=F32)
    cmp_end = lax.broadcasted_iota(jnp.int32, (n_cmp, 1), 0) * CMP_STRIDE + (CMP_LEN - 1)
    sc = jnp.where(cmp_end <= t_col, sc, NEG_INF)
    e = jnp.exp2(sc - jnp.max(sc, axis=0, keepdims=True))
    l = jnp.sum(e, axis=0, keepdims=True)
    pc = e * jnp.where(t_col >= CMP_LEN - 1, 1.0 / l, 0.0)
    oc = jnp.dot(vct_ref[0], pc.astype(BF16), preferred_element_type=F32)

    imp_c = pc[:, 0:Q_BLOCK]
    for r in range(1, NSA_GROUP):
        imp_c = imp_c + pc[:, r * Q_BLOCK:(r + 1) * Q_BLOCK]
    sj = lax.broadcasted_iota(jnp.int32, (n_sel, n_cmp), 0) * SEL_LEN
    cn = lax.broadcasted_iota(jnp.int32, (n_sel, n_cmp), 1) * CMP_STRIDE
    overlap = jnp.where((cn <= sj + SEL_LEN - 1) & (cn + CMP_LEN - 1 >= sj), 1.0, 0.0).astype(BF16)
    imp = jnp.zeros((n_sel, Q_BLOCK), F32)
    for part in _split3(imp_c):
        imp = imp + jnp.dot(overlap, part, preferred_element_type=F32)

    t_win = t_col + (jnp.min(imp, keepdims=True) < 0.0).astype(jnp.int32)
    wlen = WINDOW + Q_BLOCK
    w0 = pl.multiple_of(jnp.maximum(s0 - WINDOW, 0), Q_BLOCK)
    wc = w0 >> (T_CHUNK.bit_length() - 1)
    sw = jnp.dot(kw_ref[0, pl.ds(w0, wlen), :], qt, preferred_element_type=F32)
    wpos = w0 + lax.broadcasted_iota(jnp.int32, (wlen, 1), 0)
    sw = jnp.where(wpos <= t_win, sw, NEG_INF)
    sw = jnp.concatenate([jnp.where(wpos[0:Q_BLOCK] > t_win - WINDOW, sw[0:Q_BLOCK], NEG_INF), sw[Q_BLOCK:]],
                         axis=0)
    pw = jnp.exp2((sw - jnp.max(sw, axis=0, keepdims=True)).astype(BF16))
    accw = jnp.zeros((dh + ONES_ROWS, cols), F32)
    for j in range(wlen // T_CHUNK):
        accw = accw + jnp.dot(with_ones(vwt_ref[wc + j]), pw[j * T_CHUNK:(j + 1) * T_CHUNK],
                              preferred_element_type=F32)
    ow = accw[0:dh] * (1.0 / accw[dh:dh + 1])

    def gate_row(b):
        gate = jax.nn.sigmoid(glt_ref[0].astype(F32))
        return jnp.concatenate(
            [gate[N_BRANCH * r + b:N_BRANCH * r + b + 1] for r in range(NSA_GROUP)], axis=1)

    part_ref[...] = gate_row(0) * oc + gate_row(2) * ow


    jj = lax.broadcasted_iota(jnp.int32, (n_sel, Q_BLOCK), 0)
    cur = (s0 + lax.broadcasted_iota(jnp.int32, (1, Q_BLOCK), 1)) >> (SEL_LEN.bit_length() - 1)
    forced = (jj == 0) | (jj == cur) | (jj == cur - 1)
    future = jj > cur
    assert FORCE_BONUS > 2 * NSA_GROUP
    n_forced = 3
    work = jnp.where(forced, -jnp.inf, jnp.where(future, NEG_INF, imp))
    jf = jj.astype(F32)
    chosen = jnp.where(forced, 1.0, 0.0)
    for _ in range(min(SEL_TOPK, n_sel) - n_forced):
        mx = jnp.max(work, axis=0, keepdims=True)
        first = jnp.min(jnp.where(work == mx, jf, float(n_sel)), axis=0, keepdims=True)
        hit = jf == first
        chosen = jnp.where(hit, 1.0, chosen)
        work = jnp.where(hit, -jnp.inf, work)
    bias = jnp.where((chosen > 0.0) & jnp.logical_not(future), 0.0, NEG_INF).astype(BF16)
    qa_ref[0:n_sel, :] = jnp.concatenate([bias] * NSA_GROUP, axis=1)
    qa_ref[n_sel:, :] = qt

    m_ref[...] = jnp.full(m_ref.shape, NEG_INF, F32)
    acc_ref[...] = jnp.zeros(acc_ref.shape, F32)

    def scores(c, dst_ref):
        k0 = pl.multiple_of(c * SEL_CHUNK, SEL_CHUNK)
        ka = jnp.concatenate([oh_ref[pl.ds(k0, SEL_CHUNK), :], ks_ref[0, pl.ds(k0, SEL_CHUNK), :]],
                             axis=1)
        for h0 in range(0, cols, pair):
            dst_ref[:, h0:h0 + pair] = jnp.dot(ka, qa_ref[:, h0:h0 + pair], preferred_element_type=F32)

    def accumulate(c, src_ref, causal, n_keys=SEL_CHUNK):
        s = src_ref[0:n_keys, :]
        if causal:
            kpos = c * SEL_CHUNK + lax.broadcasted_iota(jnp.int32, (n_keys, 1), 0)
            s = jnp.where(kpos <= t_col, s, NEG_INF)
        m_old = m_ref[...]
        m_new = jnp.maximum(m_old, jnp.max(s, axis=0, keepdims=True))
        p = jnp.exp2(s - m_new).astype(BF16)
        c0 = c * (SEL_CHUNK // T_CHUNK)
        vt = jnp.concatenate([vst_ref[c0 + k] for k in range(n_keys // T_CHUNK)], axis=1)
        pv = jnp.dot(with_ones(vt), p, preferred_element_type=F32)
        acc_ref[...] = acc_ref[...] * jnp.exp2(m_old - m_new) + pv
        m_ref[...] = m_new

    n_full = s0 >> (SEL_CHUNK.bit_length() - 1)
    sa_ref, sb_ref = s_ref.at[0], s_ref.at[1]
    scores(0, sa_ref)

    def body(j, carry):
        c = 2 * j
        scores(c + 1, sb_ref)
        accumulate(c, sa_ref, False)
        scores(c + 2, sa_ref)
        accumulate(c + 1, sb_ref, False)
        return carry

    lax.fori_loop(0, n_full >> 1, body, 0)

    @pl.when((n_full & 1) == 1)
    def _():
        scores(n_full, sb_ref)
        accumulate(n_full - 1, sa_ref, False)

    own = s0 - n_full * SEL_CHUNK
    for k in range(SEL_CHUNK // Q_BLOCK):
        @pl.when(own == k * Q_BLOCK)
        def _(k=k):
            accumulate(n_full, s_ref.at[n_full & 1], True, (k + 1) * Q_BLOCK)

    acc = acc_ref[...]
    os_ = acc[0:dh] * (1.0 / acc[dh:dh + 1])

    out_t = part_ref[...] + gate_row(1) * os_
    out_t = jnp.concatenate([out_t[:, r * Q_BLOCK:(r + 1) * Q_BLOCK] for r in range(NSA_GROUP)], axis=0)
    o_ref[...] = out_t.T.astype(o_ref.dtype)


def nsa_attention(proj_t, rows, kc, vct, onehot, ks, kw):
    nt = proj_t.shape[0]
    s = nt * T_CHUNK
    g, _, dh = kc.shape
    q_row, vs_row, vw_row, gate_row = rows
    assert Q_BLOCK == T_CHUNK and all(r % (NSA_GROUP * dh) == 0 for r in (q_row, vs_row, vw_row))
    assert gate_row % GATE_SLOT == 0
    dv = dh + ONES_ROWS
    cols = NSA_GROUP * Q_BLOCK
    whole = lambda a: pl.BlockSpec((1,) + a.shape[1:], lambda gi, i: (gi,) + (0,) * (a.ndim - 1))
    v_t = lambda row: pl.BlockSpec((nt, dh, T_CHUNK), lambda gi, i: (0, row // dh + gi, 0))
    return pl.pallas_call(
        functools.partial(_nsa_kernel, seq=s),
        grid=(g, s // Q_BLOCK),
        in_specs=[
            pl.BlockSpec((1, NSA_GROUP * dh, Q_BLOCK), lambda gi, i: (i, q_row // (NSA_GROUP * dh) + gi, 0)),
            whole(kc), whole(vct), pl.BlockSpec(onehot.shape, lambda gi, i: (0, 0)),
            whole(ks), v_t(vs_row), whole(kw), v_t(vw_row),
            pl.BlockSpec((1, GATE_SLOT, Q_BLOCK), lambda gi, i: (i, gate_row // GATE_SLOT + gi, 0)),
        ],
        out_specs=pl.BlockSpec((Q_BLOCK, NSA_GROUP * dh), lambda gi, i: (i, gi)),
        out_shape=jax.ShapeDtypeStruct((s, g * NSA_GROUP * dh), BF16),
        scratch_shapes=[pltpu.VMEM((onehot.shape[1] + dh, cols), BF16),
                        pltpu.VMEM((1, cols), F32),
                        pltpu.VMEM((dv, cols), F32),
                        pltpu.VMEM((2, SEL_CHUNK, cols), F32),
                        pltpu.VMEM((dh, cols), F32)],
        compiler_params=_params("parallel", "arbitrary"),
        name="nsa_attention",
    )(proj_t, kc, vct, onehot, ks, proj_t, kw, proj_t, proj_t)


def _out_proj_kernel(c_ref, o_ref, w_ref, x_ref, y_ref):
    half = c_ref.shape[1]
    acc = jnp.dot(c_ref[...], w_ref[0:half, :], preferred_element_type=F32)
    acc = acc + jnp.dot(o_ref[...], w_ref[half:, :], preferred_element_type=F32)
    y_ref[...] = x_ref[...] + acc


def out_proj(c, o, w, x, *, tm=1024, tn=512):
    m, half = c.shape
    n = w.shape[1]
    return pl.pallas_call(
        _out_proj_kernel,
        grid=(m // tm, n // tn),
        in_specs=[
            pl.BlockSpec((tm, half), lambda i, j: (i, 0)),
            pl.BlockSpec((tm, half), lambda i, j: (i, 0)),
            pl.BlockSpec((2 * half, tn), lambda i, j: (0, j)),
            pl.BlockSpec((tm, tn), lambda i, j: (i, j)),
        ],
        out_specs=pl.BlockSpec((tm, tn), lambda i, j: (i, j)),
        out_shape=jax.ShapeDtypeStruct((m, n), F32),
        compiler_params=_params("parallel", "parallel"),
        name="out_proj",
    )(c, o, w, x)


def _pool_kernel(x_ref, halo_ref, g_ref, w_ref, sc_ref, o_ref, *, tm):
    i = pl.program_id(0)
    x = x_ref[...]
    xn = _rms(x, g_ref[...])
    hn = _rms(halo_ref[...], g_ref[...]) * jnp.where(i > 0, 1.0, 0.0)
    ext = jnp.concatenate([hn, xn], axis=0)
    pd = w_ref.shape[1]
    pos1 = i * tm + 1 + lax.broadcasted_iota(jnp.int32, (tm, 1), 0)
    for gi, win in enumerate(POOL_WINDOWS):
        sl = slice(gi * pd, (gi + 1) * pd)
        s = ext[:, sl]
        sh = 1
        while sh < win:
            s = s + pltpu.roll(s, sh, axis=0)
            sh *= 2
        cnt = jnp.minimum(pos1, win).astype(F32)
        p = s[POOL_HALO:] / cnt - xn[:, sl]
        y = jnp.dot(p.astype(BF16), w_ref[gi], preferred_element_type=F32)
        o_ref[:, sl] = x[:, sl] + y * sc_ref[:, sl]


def pool_mixer(x, g, pool_w, pool_scale, *, tm=512):
    m, d = x.shape
    ng, pd, _ = pool_w.shape
    hb = tm // POOL_HALO
    return pl.pallas_call(
        functools.partial(_pool_kernel, tm=tm),
        grid=(m // tm,),
        in_specs=[
            pl.BlockSpec((tm, d), lambda i: (i, 0)),
            pl.BlockSpec((POOL_HALO, d), lambda i: (jnp.maximum(i * hb - 1, 0), 0)),
            pl.BlockSpec((1, d), lambda i: (0, 0)),
            pl.BlockSpec((ng, pd, pd), lambda i: (0, 0, 0)),
            pl.BlockSpec((1, d), lambda i: (0, 0)),
        ],
        out_specs=pl.BlockSpec((tm, d), lambda i: (i, 0)),
        out_shape=jax.ShapeDtypeStruct((m, d), F32),
        compiler_params=_params("parallel"),
        name="pool_mixer",
    )(x, x, g.reshape(1, d), pool_w, pool_scale.reshape(1, d))


def _even_mixer(x, g, w_in, w_out, conv_w, conv_b, ln_g, ln_b, pos_k, pos_v, kw1, kw2, vw1, vw2):
    s, d = x.shape
    dh, gk, hq = NSA_HEAD_DIM, NSA_KV_HEADS, NSA_HEADS
    conv_cols = 2 * conv_w.shape[1]
    q_cols, kv_cols = hq * dh, gk * dh
    tn = 512
    q0 = conv_cols
    kv = lambda k: w_in[:, q0 + q_cols + k * kv_cols:q0 + q_cols + (k + 1) * kv_cols]
    w_gate = w_in[:, q0 + q_cols + 6 * kv_cols:].reshape(d, gk, NSA_GROUP * N_BRANCH)
    w_gate = jnp.pad(w_gate, ((0, 0), (0, 0), (0, GATE_SLOT - NSA_GROUP * N_BRANCH))).reshape(d, gk * GATE_SLOT)
    nat = [w_in[:, :q0], kv(0), kv(1), kv(2), kv(4)]
    feat = [w_in[:, q0:q0 + q_cols], kv(3), kv(5), w_gate]
    n_nat_cols = sum(w.shape[1] for w in nat)
    n_feat_cols = sum(w.shape[1] for w in feat)
    assert n_nat_cols % tn == 0
    w_all = jnp.concatenate(nat + feat + [jnp.zeros((d, (-n_feat_cols) % tn), F32)], axis=1).astype(BF16)

    proj, proj_t = in_projection(x, g, w_all, n_nat_cols // tn, tn=tn)
    c = conv_module(proj, conv_w, conv_b, ln_g, ln_b)

    kv0 = conv_cols
    hm = lambda k: proj[:, kv0 + k * kv_cols:kv0 + (k + 1) * kv_cols].reshape(s, gk, dh).transpose(1, 0, 2)
    k_s, k_w = hm(2), hm(3)
    nch = s // CMP_STRIDE
    kt = jnp.stack([hm(0), hm(1)]).reshape(2, gk, nch, CMP_STRIDE * dh)
    pos = jnp.stack([pos_k, pos_v]).reshape(2, 2, CMP_STRIDE * dh)
    kvc = compress(kt, pos, jnp.stack([kw1, vw1]).astype(BF16), jnp.stack([kw2, vw2]).astype(BF16))

    n_sel = s // SEL_LEN
    onehot = (jnp.arange(s)[:, None] // SEL_LEN == jnp.arange(n_sel)[None, :]).astype(BF16)
    rows = (0, q_cols, q_cols + kv_cols, q_cols + 2 * kv_cols)
    o = nsa_attention(proj_t, rows, kvc[0], kvc[1].transpose(0, 2, 1), onehot, k_s, k_w)
    return out_proj(c, o, w_out.astype(BF16), x)


def kernel(x, mix_norm, mlp_norm, w_mlp_in, w_mlp_out, w_in, w_out, conv_w, conv_b, conv_ln_g,
           conv_ln_b, cmp_pos_k, cmp_pos_v, cmp_k_w1, cmp_k_w2, cmp_v_w1, cmp_v_w2, pool_w,
           pool_scale, final_norm):
    b, s, d = x.shape
    assert b == 1
    depth = mix_norm.shape[0]
    h = x.reshape(s, d)
    for layer in range(depth):
        i = layer // 2
        if layer % 2 == 0:
            h = _even_mixer(h, mix_norm[layer], w_in[i], w_out[i], conv_w[i], conv_b[i],
                            conv_ln_g[i], conv_ln_b[i], cmp_pos_k[i], cmp_pos_v[i],
                            cmp_k_w1[i], cmp_k_w2[i], cmp_v_w1[i], cmp_v_w2[i])
        else:
            h = pool_mixer(h, mix_norm[layer], pool_w[i].astype(BF16), pool_scale[i])
        fg = final_norm if layer == depth - 1 else None
        h = fused_mlp(h, mlp_norm[layer], w_mlp_in, w_mlp_out, layer, fg)
    return h.reshape(b, s, d)
```

```python
import functools

import jax
import jax.numpy as jnp
from jax import lax
from jax.experimental import pallas as pl
from jax.experimental.pallas import tpu as pltpu

F32 = jnp.float32
BF16 = jnp.bfloat16

EPS = 1e-6
NEG_INF = -1e30
LOG2E = 1.4426950408889634
CONV_WIDTH = 31
SUBLANES = 8
CONV_HALO = 32
NSA_HEADS = 16
NSA_HEAD_DIM = 64
NSA_KV_HEADS = 4
NSA_GROUP = NSA_HEADS // NSA_KV_HEADS
N_BRANCH = 3
CMP_LEN = 32
CMP_STRIDE = 16
SEL_LEN = 64
SEL_TOPK = 16
WINDOW = 512
Q_BLOCK = 256
FORCE_BONUS = 1e3
POOL_WINDOWS = (2, 4, 8, 16)
POOL_HALO = 16
SEL_CHUNK = 512
ONES_ROWS = 16
T_CHUNK = 256
GATE_SLOT = 16

VMEM_LIMIT = 56 * 1024 * 1024


def _params(*sem):
    return pltpu.CompilerParams(dimension_semantics=sem, vmem_limit_bytes=VMEM_LIMIT)


def _rms(x, g):
    r = lax.rsqrt(jnp.mean(x * x, axis=-1, keepdims=True) + EPS)
    return x * r * g


def _in_proj_kernel(x_ref, g_ref, w_ref, on_ref, oh_ref, ot_ref, xn_ref, *, n_nat, n_head):
    j = pl.program_id(1)

    @pl.when(j == 0)
    def _():
        xn_ref[...] = _rms(x_ref[...], g_ref[...]).astype(BF16)

    acc = jnp.dot(xn_ref[...], w_ref[...], preferred_element_type=F32)

    @pl.when(j < n_nat)
    def _():
        on_ref[...] = acc.astype(on_ref.dtype)

    @pl.when((j >= n_nat) & (j < n_nat + n_head))
    def _():
        dh = oh_ref.shape[3]
        for t in range(oh_ref.shape[0]):
            for gi in range(oh_ref.shape[1]):
                c0 = (t * oh_ref.shape[1] + gi) * dh
                oh_ref[t, gi] = acc[:, c0:c0 + dh].astype(oh_ref.dtype)

    @pl.when(j >= n_nat + n_head)
    def _():
        acc_t = acc.T
        for k in range(ot_ref.shape[0]):
            ot_ref[k] = acc_t[:, k * T_CHUNK:(k + 1) * T_CHUNK].astype(ot_ref.dtype)


def in_projection(x, g, w, n_nat, n_head, groups, dh, *, tm=1024, tn=512):
    m, k = x.shape
    n_t = w.shape[1] // tn - n_nat - n_head
    per_tile = tn // (groups * dh)
    assert w.shape[1] % tn == 0 and m % tm == 0 and tm % T_CHUNK == 0 and tn % (groups * dh) == 0
    return pl.pallas_call(
        functools.partial(_in_proj_kernel, n_nat=n_nat, n_head=n_head),
        grid=(m // tm, n_nat + n_head + n_t),
        in_specs=[
            pl.BlockSpec((tm, k), lambda i, j: (i, 0)),
            pl.BlockSpec((1, k), lambda i, j: (0, 0)),
            pl.BlockSpec((k, tn), lambda i, j: (0, j)),
        ],
        out_specs=[
            pl.BlockSpec((tm, tn), lambda i, j: (i, jnp.minimum(j, n_nat - 1))),
            pl.BlockSpec((per_tile, groups, tm, dh), lambda i, j: (jnp.clip(j - n_nat, 0, n_head - 1), 0, i, 0)),
            pl.BlockSpec((tm // T_CHUNK, tn, T_CHUNK),
                         lambda i, j: (i, jnp.maximum(j - n_nat - n_head, 0), 0)),
        ],
        out_shape=[jax.ShapeDtypeStruct((m, n_nat * tn), BF16),
                   jax.ShapeDtypeStruct((n_head * per_tile, groups, m, dh), BF16),
                   jax.ShapeDtypeStruct((m // T_CHUNK, n_t * tn, T_CHUNK), BF16)],
        scratch_shapes=[pltpu.VMEM((tm, k), BF16)],
        compiler_params=_params("parallel", "arbitrary"),
        name="in_projection",
    )(x, g.reshape(1, k), w)


def _mlp_kernel(x_ref, g_ref, w1_ref, w2_ref, gf_ref, o_ref, xn_ref, h_ref, *, final_norm):
    f = pl.program_id(1)

    @pl.when(f == 0)
    def _():
        x = x_ref[...]
        xn_ref[...] = _rms(x, g_ref[...]).astype(BF16)
        o_ref[...] = x
        h_ref[...] = jnp.zeros(h_ref.shape, BF16)

    h_new = jnp.dot(xn_ref[...], w1_ref[...].astype(BF16), preferred_element_type=F32)
    o_ref[...] += jnp.dot(h_ref[...], w2_ref[...].astype(BF16), preferred_element_type=F32)
    h_ref[...] = jnp.square(jnp.maximum(h_new, 0.0)).astype(BF16)

    if final_norm:
        @pl.when(f == pl.num_programs(1) - 1)
        def _():
            o_ref[...] = _rms(o_ref[...], gf_ref[...])


def fused_mlp(x, g, w1, w2, layer, gf=None, *, tm=2048, tf=256):
    m, d = x.shape
    nf = w1.shape[2] // tf
    final_norm = gf is not None
    if gf is None:
        gf = g
    once = pl.Buffered(1)
    return pl.pallas_call(
        functools.partial(_mlp_kernel, final_norm=final_norm),
        grid=(m // tm, nf + 1),
        in_specs=[
            pl.BlockSpec((tm, d), lambda i, f: (i, 0), pipeline_mode=once),
            pl.BlockSpec((1, d), lambda i, f: (0, 0)),
            pl.BlockSpec((None, d, tf), lambda i, f: (layer, 0, jnp.minimum(f, nf - 1))),
            pl.BlockSpec((None, tf, d), lambda i, f: (layer, jnp.maximum(f - 1, 0), 0)),
            pl.BlockSpec((1, d), lambda i, f: (0, 0)),
        ],
        out_specs=pl.BlockSpec((tm, d), lambda i, f: (i, 0), pipeline_mode=once),
        out_shape=jax.ShapeDtypeStruct((m, d), F32),
        scratch_shapes=[pltpu.VMEM((tm, d), BF16), pltpu.VMEM((tm, tf), BF16)],
        compiler_params=_params("parallel", "arbitrary"),
        name="fused_mlp",
    )(x, g.reshape(1, d), w1, w2, gf.reshape(1, d))


def _conv_kernel(a_ref, gt_ref, w_ref, b_ref, lg_ref, lb_ref, o_ref, cs_ref, y_ref, wb_ref, *, tt):
    i = pl.program_id(0)

    @pl.when(i == 0)
    def _():
        cs_ref[0, 0:CONV_HALO, :] = jnp.zeros((CONV_HALO, cs_ref.shape[2]), F32)
        for k in range(CONV_WIDTH):
            wb_ref[k] = jnp.broadcast_to(w_ref[k:k + 1, :], wb_ref.shape[1:])

    @pl.when(i > 0)
    def _():
        cs_ref[0, 0:CONV_HALO, :] = cs_ref[0, tt:tt + CONV_HALO, :]

    cs_ref[0, CONV_HALO:, :] = a_ref[...].astype(F32) * jax.nn.sigmoid(gt_ref[...].astype(F32))
    c_ext = cs_ref[0]
    for b in range(1, SUBLANES):
        cs_ref[b] = pltpu.roll(c_ext, b, axis=0)

    rows = 4 * SUBLANES

    def chunk(j, carry):
        r0 = pl.multiple_of(j * rows, rows)
        groups = range(rows // SUBLANES)
        acc = [jnp.zeros((SUBLANES, cs_ref.shape[2]), F32) + b_ref[...] for _ in groups]
        for k in range(CONV_WIDTH):
            a, b = divmod(k, SUBLANES)
            wk = wb_ref[CONV_WIDTH - 1 - k]
            for q in groups:
                lo = pl.multiple_of(r0 + CONV_HALO + (q - a) * SUBLANES, SUBLANES)
                acc[q] = acc[q] + cs_ref[b, pl.ds(lo, SUBLANES), :] * wk
        for q in groups:
            y_ref[pl.ds(pl.multiple_of(r0 + q * SUBLANES, SUBLANES), SUBLANES), :] = acc[q]
        return carry

    lax.fori_loop(0, tt // rows, chunk, 0)

    acc = y_ref[...]
    mu = jnp.mean(acc, axis=-1, keepdims=True)
    xc = acc - mu
    var = jnp.mean(xc * xc, axis=-1, keepdims=True)
    y = xc * lax.rsqrt(var + EPS) * lg_ref[...] + lb_ref[...]
    o_ref[...] = (y * jax.nn.sigmoid(y)).astype(o_ref.dtype)


def conv_module(proj, conv_w, conv_b, ln_g, ln_b, *, tt=512):
    s = proj.shape[0]
    c = conv_w.shape[1]
    row = lambda v: v.reshape(1, c)
    return pl.pallas_call(
        functools.partial(_conv_kernel, tt=tt),
        grid=(s // tt,),
        in_specs=[
            pl.BlockSpec((tt, c), lambda i: (i, 0)),
            pl.BlockSpec((tt, c), lambda i: (i, 1)),
            pl.BlockSpec((CONV_WIDTH, c), lambda i: (0, 0)),
            pl.BlockSpec((1, c), lambda i: (0, 0)),
            pl.BlockSpec((1, c), lambda i: (0, 0)),
            pl.BlockSpec((1, c), lambda i: (0, 0)),
        ],
        out_specs=pl.BlockSpec((tt, c), lambda i: (i, 0)),
        out_shape=jax.ShapeDtypeStruct((s, c), BF16),
        scratch_shapes=[pltpu.VMEM((SUBLANES, tt + CONV_HALO, c), F32), pltpu.VMEM((tt, c), F32),
                        pltpu.VMEM((CONV_WIDTH, SUBLANES, c), F32)],
        compiler_params=_params("arbitrary"),
        name="conv_module",
    )(proj, proj, conv_w, row(conv_b), row(ln_g), row(ln_b))


def _compress_kernel(kt_ref, pos_ref, w1_ref, w2_ref, o_ref):
    kt = kt_ref[0, 0].astype(F32)
    half = kt.shape[1]
    first = (kt + pos_ref[0, 0:1, :]).astype(BF16)
    second = (kt + pos_ref[0, 1:2, :]).astype(BF16)
    p = jnp.dot(first, w1_ref[0, 0:half, :], preferred_element_type=F32)
    q = jnp.dot(second, w1_ref[0, half:, :], preferred_element_type=F32)
    n = q.shape[0]
    h = p + pltpu.roll(q, n - 1, axis=0)
    h = h * jax.nn.sigmoid(h)
    o_ref[0, 0] = jnp.dot(h.astype(BF16), w2_ref[0], preferred_element_type=F32).astype(o_ref.dtype)


def compress(kt, pos, w1, w2):
    two, g, nch, half = kt.shape
    hid = w1.shape[2]
    dh = w2.shape[2]
    return pl.pallas_call(
        _compress_kernel,
        grid=(two, g),
        in_specs=[
            pl.BlockSpec((1, 1, nch, half), lambda a, b: (a, b, 0, 0)),
            pl.BlockSpec((1, 2, half), lambda a, b: (a, 0, 0)),
            pl.BlockSpec((1, 2 * half, hid), lambda a, b: (a, 0, 0)),
            pl.BlockSpec((1, hid, dh), lambda a, b: (a, 0, 0)),
        ],
        out_specs=pl.BlockSpec((1, 1, nch, dh), lambda a, b: (a, b, 0, 0)),
        out_shape=jax.ShapeDtypeStruct((two, g, nch, dh), BF16),
        compiler_params=_params("parallel", "parallel"),
        name="compress",
    )(kt, pos, w1, w2)


def _split3(x):
    hi = x.astype(BF16)
    r1 = x - hi.astype(F32)
    mid = r1.astype(BF16)
    lo = (r1 - mid.astype(F32)).astype(BF16)
    return hi, mid, lo


def _nsa_kernel(qt_ref, kc_ref, vct_ref, oh_ref, ks_ref, vst_ref, kw_ref, vwt_ref, glt_ref, o_ref,
                qa_ref, m_ref, acc_ref, s_ref, part_ref, *, seq):
    i = pl.program_id(1)
    s0 = i * Q_BLOCK
    dh = NSA_HEAD_DIM
    cols = NSA_GROUP * Q_BLOCK
    pair = 2 * Q_BLOCK
    n_sel = seq // SEL_LEN
    n_cmp = kc_ref.shape[1]

    def with_ones(vt):
        return jnp.concatenate([vt, jnp.ones((ONES_ROWS, vt.shape[1]), BF16)], axis=0)

    qt = jnp.concatenate([qt_ref[0, r * dh:(r + 1) * dh, :] for r in range(NSA_GROUP)], axis=1)
    qt = (qt.astype(F32) * (dh ** -0.5 * LOG2E)).astype(BF16)
    t_col = s0 + (lax.broadcasted_iota(jnp.int32, (1, cols), 1) & (Q_BLOCK - 1))

    sc = jnp.dot(kc_ref[0], qt, preferred_element_type=F32)
    cmp_end = lax.broadcasted_iota(jnp.int32, (n_cmp, 1), 0) * CMP_STRIDE + (CMP_LEN - 1)
    sc = jnp.where(cmp_end <= t_col, sc, NEG_INF)
    e = jnp.exp2(sc - jnp.max(sc, axis=0, keepdims=True))
    l = jnp.sum(e, axis=0, keepdims=True)
    pc = e * jnp.where(t_col >= CMP_LEN - 1, 1.0 / l, 0.0)
    oc = jnp.dot(vct_ref[0], pc.astype(BF16), preferred_element_type=F32)

    imp_c = pc[:, 0:Q_BLOCK]
    for r in range(1, NSA_GROUP):
        imp_c = imp_c + pc[:, r * Q_BLOCK:(r + 1) * Q_BLOCK]
    sj = lax.broadcasted_iota(jnp.int32, (n_sel, n_cmp), 0) * SEL_LEN
    cn = lax.broadcasted_iota(jnp.int32, (n_sel, n_cmp), 1) * CMP_STRIDE
    overlap = jnp.where((cn <= sj + SEL_LEN - 1) & (cn + CMP_LEN - 1 >= sj), 1.0, 0.0).astype(BF16)
    imp = jnp.zeros((n_sel, Q_BLOCK), F32)
    for part in _split3(imp_c):
        imp = imp + jnp.dot(overlap, part, preferred_element_type=F32)

    t_win = t_col + (jnp.min(imp, keepdims=True) < 0.0).astype(jnp.int32)
    wlen = WINDOW + Q_BLOCK
    w0 = pl.multiple_of(jnp.maximum(s0 - WINDOW, 0), Q_BLOCK)
    wc = w0 >> (T_CHUNK.bit_length() - 1)
    sw = jnp.dot(kw_ref[0, pl.ds(w0, wlen), :], qt, preferred_element_type=F32)
    wpos = w0 + lax.broadcasted_iota(jnp.int32, (wlen, 1), 0)
    sw = jnp.where(wpos <= t_win, sw, NEG_INF)
    sw = jnp.concatenate([jnp.where(wpos[0:Q_BLOCK] > t_win - WINDOW, sw[0:Q_BLOCK], NEG_INF), sw[Q_BLOCK:]],
                         axis=0)
    pw = jnp.exp2((sw - jnp.max(sw, axis=0, keepdims=True)).astype(BF16))
    accw = jnp.zeros((dh + ONES_ROWS, cols), F32)
    for j in range(wlen // T_CHUNK):
        accw = accw + jnp.dot(with_ones(vwt_ref[wc + j]), pw[j * T_CHUNK:(j + 1) * T_CHUNK],
                              preferred_element_type=F32)
    ow = accw[0:dh] * (1.0 / accw[dh:dh + 1])

    def gate_row(b):
        gate = jax.nn.sigmoid(glt_ref[0].astype(F32))
        return jnp.concatenate(
            [gate[N_BRANCH * r + b:N_BRANCH * r + b + 1] for r in range(NSA_GROUP)], axis=1)

    part_ref[...] = gate_row(0) * oc + gate_row(2) * ow


    jj = lax.broadcasted_iota(jnp.int32, (n_sel, Q_BLOCK), 0)
    cur = (s0 + lax.broadcasted_iota(jnp.int32, (1, Q_BLOCK), 1)) >> (SEL_LEN.bit_length() - 1)
    forced = (jj == 0) | (jj == cur) | (jj == cur - 1)
    future = jj > cur
    assert FORCE_BONUS > 2 * NSA_GROUP
    n_forced = 3
    work = jnp.where(forced, -jnp.inf, jnp.where(future, NEG_INF, imp))
    jf = jj.astype(F32)
    chosen = jnp.where(forced, 1.0, 0.0)
    for _ in range(min(SEL_TOPK, n_sel) - n_forced):
        mx = jnp.max(work, axis=0, keepdims=True)
        first = jnp.min(jnp.where(work == mx, jf, float(n_sel)), axis=0, keepdims=True)
        hit = jf == first
        chosen = jnp.where(hit, 1.0, chosen)
        work = jnp.where(hit, -jnp.inf, work)
    bias = jnp.where((chosen > 0.0) & jnp.logical_not(future), 0.0, NEG_INF).astype(BF16)
    qa_ref[0:n_sel, :] = jnp.concatenate([bias] * NSA_GROUP, axis=1)
    qa_ref[n_sel:, :] = qt

    m_ref[...] = jnp.full(m_ref.shape, NEG_INF, F32)
    acc_ref[...] = jnp.zeros(acc_ref.shape, F32)

    def scores(c, dst_ref):
        k0 = pl.multiple_of(c * SEL_CHUNK, SEL_CHUNK)
        ka = jnp.concatenate([oh_ref[pl.ds(k0, SEL_CHUNK), :], ks_ref[0, pl.ds(k0, SEL_CHUNK), :]],
                             axis=1)
        for h0 in range(0, cols, pair):
            dst_ref[:, h0:h0 + pair] = jnp.dot(ka, qa_ref[:, h0:h0 + pair], preferred_element_type=F32)

    def accumulate(c, src_ref, causal, n_keys=SEL_CHUNK):
        s = src_ref[0:n_keys, :]
        if causal:
            kpos = c * SEL_CHUNK + lax.broadcasted_iota(jnp.int32, (n_keys, 1), 0)
            s = jnp.where(kpos <= t_col, s, NEG_INF)
        m_old = m_ref[...]
        m_new = jnp.maximum(m_old, jnp.max(s, axis=0, keepdims=True))
        p = jnp.exp2(s - m_new).astype(BF16)
        c0 = c * (SEL_CHUNK // T_CHUNK)
        vt = jnp.concatenate([vst_ref[c0 + k] for k in range(n_keys // T_CHUNK)], axis=1)
        pv = jnp.dot(with_ones(vt), p, preferred_element_type=F32)
        acc_ref[...] = acc_ref[...] * jnp.exp2(m_old - m_new) + pv
        m_ref[...] = m_new

    n_full = s0 >> (SEL_CHUNK.bit_length() - 1)
    sa_ref, sb_ref = s_ref.at[0], s_ref.at[1]
    scores(0, sa_ref)

    def body(j, carry):
        c = 2 * j
        scores(c + 1, sb_ref)
        accumulate(c, sa_ref, False)
        scores(c + 2, sa_ref)
        accumulate(c + 1, sb_ref, False)
        return carry

    lax.fori_loop(0, n_full >> 1, body, 0)

    @pl.when((n_full & 1) == 1)
    def _():
        scores(n_full, sb_ref)
        accumulate(n_full - 1, sa_ref, False)

    own = s0 - n_full * SEL_CHUNK
    for k in range(SEL_CHUNK // Q_BLOCK):
        @pl.when(own == k * Q_BLOCK)
        def _(k=k):
            accumulate(n_full, s_ref.at[n_full & 1], True, (k + 1) * Q_BLOCK)

    acc = acc_ref[...]
    os_ = acc[0:dh] * (1.0 / acc[dh:dh + 1])

    out_t = part_ref[...] + gate_row(1) * os_
    out_t = jnp.concatenate([out_t[:, r * Q_BLOCK:(r + 1) * Q_BLOCK] for r in range(NSA_GROUP)], axis=0)
    o_ref[...] = out_t.T.astype(o_ref.dtype)


def nsa_attention(proj_t, rows, kc, vct, onehot, ks, kw):
    nt = proj_t.shape[0]
    s = nt * T_CHUNK
    g, _, dh = kc.shape
    q_row, vs_row, vw_row, gate_row = rows
    assert Q_BLOCK == T_CHUNK and all(r % (NSA_GROUP * dh) == 0 for r in (q_row, vs_row, vw_row))
    assert gate_row % GATE_SLOT == 0
    dv = dh + ONES_ROWS
    cols = NSA_GROUP * Q_BLOCK
    whole = lambda a: pl.BlockSpec((1,) + a.shape[1:], lambda gi, i: (gi,) + (0,) * (a.ndim - 1))
    v_t = lambda row: pl.BlockSpec((nt, dh, T_CHUNK), lambda gi, i: (0, row // dh + gi, 0))
    return pl.pallas_call(
        functools.partial(_nsa_kernel, seq=s),
        grid=(g, s // Q_BLOCK),
        in_specs=[
            pl.BlockSpec((1, NSA_GROUP * dh, Q_BLOCK), lambda gi, i: (i, q_row // (NSA_GROUP * dh) + gi, 0)),
            whole(kc), whole(vct), pl.BlockSpec(onehot.shape, lambda gi, i: (0, 0)),
            whole(ks), v_t(vs_row), whole(kw), v_t(vw_row),
            pl.BlockSpec((1, GATE_SLOT, Q_BLOCK), lambda gi, i: (i, gate_row // GATE_SLOT + gi, 0)),
        ],
        out_specs=pl.BlockSpec((Q_BLOCK, NSA_GROUP * dh), lambda gi, i: (i, gi)),
        out_shape=jax.ShapeDtypeStruct((s, g * NSA_GROUP * dh), BF16),
        scratch_shapes=[pltpu.VMEM((onehot.shape[1] + dh, cols), BF16),
                        pltpu.VMEM((1, cols), F32),
                        pltpu.VMEM((dv, cols), F32),
                        pltpu.VMEM((2, SEL_CHUNK, cols), F32),
                        pltpu.VMEM((dh, cols), F32)],
        compiler_params=_params("parallel", "arbitrary"),
        name="nsa_attention",
    )(proj_t, kc, vct, onehot, ks, proj_t, kw, proj_t, proj_t)


def _out_proj_kernel(c_ref, o_ref, w_ref, x_ref, y_ref):
    half = c_ref.shape[1]
    acc = jnp.dot(c_ref[...], w_ref[0:half, :], preferred_element_type=F32)
    acc = acc + jnp.dot(o_ref[...], w_ref[half:, :], preferred_element_type=F32)
    y_ref[...] = x_ref[...] + acc


def out_proj(c, o, w, x, *, tm=1024, tn=512):
    m, half = c.shape
    n = w.shape[1]
    return pl.pallas_call(
        _out_proj_kernel,
        grid=(m // tm, n // tn),
        in_specs=[
            pl.BlockSpec((tm, half), lambda i, j: (i, 0)),
            pl.BlockSpec((tm, half), lambda i, j: (i, 0)),
            pl.BlockSpec((2 * half, tn), lambda i, j: (0, j)),
            pl.BlockSpec((tm, tn), lambda i, j: (i, j)),
        ],
        out_specs=pl.BlockSpec((tm, tn), lambda i, j: (i, j)),
        out_shape=jax.ShapeDtypeStruct((m, n), F32),
        compiler_params=_params("parallel", "parallel"),
        name="out_proj",
    )(c, o, w, x)


def _pool_kernel(x_ref, halo_ref, g_ref, w_ref, sc_ref, o_ref, *, tm):
    i = pl.program_id(0)
    x = x_ref[...]
    xn = _rms(x, g_ref[...])
    hn = _rms(halo_ref[...], g_ref[...]) * jnp.where(i > 0, 1.0, 0.0)
    ext = jnp.concatenate([hn, xn], axis=0)
    pd = w_ref.shape[1]
    pos1 = i * tm + 1 + lax.broadcasted_iota(jnp.int32, (tm, 1), 0)
    for gi, win in enumerate(POOL_WINDOWS):
        sl = slice(gi * pd, (gi + 1) * pd)
        s = ext[:, sl]
        sh = 1
        while sh < win:
            s = s + pltpu.roll(s, sh, axis=0)
            sh *= 2
        cnt = jnp.minimum(pos1, win).astype(F32)
        p = s[POOL_HALO:] / cnt - xn[:, sl]
        y = jnp.dot(p.astype(BF16), w_ref[gi], preferred_element_type=F32)
        o_ref[:, sl] = x[:, sl] + y * sc_ref[:, sl]


def pool_mixer(x, g, pool_w, pool_scale, *, tm=512):
    m, d = x.shape
    ng, pd, _ = pool_w.shape
    hb = tm // POOL_HALO
    return pl.pallas_call(
        functools.partial(_pool_kernel, tm=tm),
        grid=(m // tm,),
        in_specs=[
            pl.BlockSpec((tm, d), lambda i: (i, 0)),
            pl.BlockSpec((POOL_HALO, d), lambda i: (jnp.maximum(i * hb - 1, 0), 0)),
            pl.BlockSpec((1, d), lambda i: (0, 0)),
            pl.BlockSpec((ng, pd, pd), lambda i: (0, 0, 0)),
            pl.BlockSpec((1, d), lambda i: (0, 0)),
        ],
        out_specs=pl.BlockSpec((tm, d), lambda i: (i, 0)),
        out_shape=jax.ShapeDtypeStruct((m, d), F32),
        compiler_params=_params("parallel"),
        name="pool_mixer",
    )(x, x, g.reshape(1, d), pool_w, pool_scale.reshape(1, d))


def _even_mixer(x, g, w_in, w_out, conv_w, conv_b, ln_g, ln_b, pos_k, pos_v, kw1, kw2, vw1, vw2):
    s, d = x.shape
    dh, gk, hq = NSA_HEAD_DIM, NSA_KV_HEADS, NSA_HEADS
    conv_cols = 2 * conv_w.shape[1]
    q_cols, kv_cols = hq * dh, gk * dh
    tn = 512
    q0 = conv_cols
    kv = lambda k: w_in[:, q0 + q_cols + k * kv_cols:q0 + q_cols + (k + 1) * kv_cols]
    w_gate = w_in[:, q0 + q_cols + 6 * kv_cols:].reshape(d, gk, NSA_GROUP * N_BRANCH)
    w_gate = jnp.pad(w_gate, ((0, 0), (0, 0), (0, GATE_SLOT - NSA_GROUP * N_BRANCH))).reshape(d, gk * GATE_SLOT)
    head = [kv(0), kv(1), kv(2), kv(4)]
    feat = [w_in[:, q0:q0 + q_cols], kv(3), kv(5), w_gate]
    n_feat_cols = sum(w.shape[1] for w in feat)
    assert conv_cols % tn == 0 and (len(head) * kv_cols) % tn == 0
    w_all = jnp.concatenate([w_in[:, :q0]] + head + feat + [jnp.zeros((d, (-n_feat_cols) % tn), F32)],
                            axis=1).astype(BF16)

    proj, proj_h, proj_t = in_projection(x, g, w_all, conv_cols // tn, len(head) * kv_cols // tn, gk, dh, tn=tn)
    c = conv_module(proj, conv_w, conv_b, ln_g, ln_b)

    k_s, k_w = proj_h[2], proj_h[3]
    nch = s // CMP_STRIDE
    kt = proj_h[0:2].reshape(2, gk, nch, CMP_STRIDE * dh)
    pos = jnp.stack([pos_k, pos_v]).reshape(2, 2, CMP_STRIDE * dh)
    kvc = compress(kt, pos, jnp.stack([kw1, vw1]).astype(BF16), jnp.stack([kw2, vw2]).astype(BF16))

    n_sel = s // SEL_LEN
    onehot = (jnp.arange(s)[:, None] // SEL_LEN == jnp.arange(n_sel)[None, :]).astype(BF16)
    rows = (0, q_cols, q_cols + kv_cols, q_cols + 2 * kv_cols)
    o = nsa_attention(proj_t, rows, kvc[0], kvc[1].transpose(0, 2, 1), onehot, k_s, k_w)
    return out_proj(c, o, w_out.astype(BF16), x)


def kernel(x, mix_norm, mlp_norm, w_mlp_in, w_mlp_out, w_in, w_out, conv_w, conv_b, conv_ln_g,
           conv_ln_b, cmp_pos_k, cmp_pos_v, cmp_k_w1, cmp_k_w2, cmp_v_w1, cmp_v_w2, pool_w,
           pool_scale, final_norm):
    b, s, d = x.shape
    assert b == 1
    depth = mix_norm.shape[0]
    h = x.reshape(s, d)
    for layer in range(depth):
        i = layer // 2
        if layer % 2 == 0:
            h = _even_mixer(h, mix_norm[layer], w_in[i], w_out[i], conv_w[i], conv_b[i],
                            conv_ln_g[i], conv_ln_b[i], cmp_pos_k[i], cmp_pos_v[i],
                            cmp_k_w1[i], cmp_k_w2[i], cmp_v_w1[i], cmp_v_w2[i])
        else:
            h = pool_mixer(h, mix_norm[layer], pool_w[i].astype(BF16), pool_scale[i])
        fg = final_norm if layer == depth - 1 else None
        h = fused_mlp(h, mlp_norm[layer], w_mlp_in, w_mlp_out, layer, fg)
    return h.reshape(b, s, d)
```

```python
import functools

import jax
import jax.numpy as jnp
from jax import lax
from jax.experimental import pallas as pl
from jax.experimental.pallas import tpu as pltpu

F32 = jnp.float32
BF16 = jnp.bfloat16

EPS = 1e-6
NEG_INF = -1e30
LOG2E = 1.4426950408889634
CONV_WIDTH = 31
SUBLANES = 8
CONV_HALO = 32
NSA_HEADS = 16
NSA_HEAD_DIM = 64
NSA_KV_HEADS = 4
NSA_GROUP = NSA_HEADS // NSA_KV_HEADS
N_BRANCH = 3
CMP_LEN = 32
CMP_STRIDE = 16
SEL_LEN = 64
SEL_TOPK = 16
WINDOW = 512
Q_BLOCK = 256
FORCE_BONUS = 1e3
POOL_WINDOWS = (2, 4, 8, 16)
POOL_HALO = 16
SEL_CHUNK = 512
ONES_ROWS = 16
T_CHUNK = 256
GATE_SLOT = 16

VMEM_LIMIT = 56 * 1024 * 1024


def _params(*sem):
    return pltpu.CompilerParams(dimension_semantics=sem, vmem_limit_bytes=VMEM_LIMIT)


def _rms(x, g):
    r = lax.rsqrt(jnp.mean(x * x, axis=-1, keepdims=True) + EPS)
    return x * r * g


def _in_proj_kernel(x_ref, g_ref, w_ref, wx_ref, on_ref, oh_ref, ot_ref, xn_ref, wb_ref, *, n_nat, n_head):
    j = pl.program_id(1)
    last = pl.num_programs(1) - 1

    @pl.when(j == 0)
    def _():
        xn_ref[...] = _rms(x_ref[...], g_ref[...]).astype(BF16)

    @pl.when(j < last)
    def _():
        wb_ref[...] = w_ref[...].astype(BF16)

    @pl.when(j == last)
    def _():
        wb_ref[...] = wx_ref[...]

    acc = jnp.dot(xn_ref[...], wb_ref[...], preferred_element_type=F32)

    @pl.when(j < n_nat)
    def _():
        on_ref[...] = acc.astype(on_ref.dtype)

    @pl.when((j >= n_nat) & (j < n_nat + n_head))
    def _():
        dh = oh_ref.shape[3]
        for t in range(oh_ref.shape[0]):
            for gi in range(oh_ref.shape[1]):
                c0 = (t * oh_ref.shape[1] + gi) * dh
                oh_ref[t, gi] = acc[:, c0:c0 + dh].astype(oh_ref.dtype)

    @pl.when(j >= n_nat + n_head)
    def _():
        acc_t = acc.T
        for k in range(ot_ref.shape[0]):
            ot_ref[k] = acc_t[:, k * T_CHUNK:(k + 1) * T_CHUNK].astype(ot_ref.dtype)


def in_projection(x, g, w, layer, order, w_extra, n_nat, n_head, groups, dh, *, tm=1024):
    m, k = x.shape
    tn = w_extra.shape[1]
    n_t = len(order) + 1 - n_nat - n_head
    per_tile = tn // (groups * dh)
    assert m % tm == 0 and tm % T_CHUNK == 0 and tn % (groups * dh) == 0

    def src_tile(j):
        idx = order[-1]
        for jj in range(len(order) - 2, -1, -1):
            idx = jnp.where(j == jj, order[jj], idx)
        return idx

    return pl.pallas_call(
        functools.partial(_in_proj_kernel, n_nat=n_nat, n_head=n_head),
        grid=(m // tm, n_nat + n_head + n_t),
        in_specs=[
            pl.BlockSpec((tm, k), lambda i, j: (i, 0)),
            pl.BlockSpec((1, k), lambda i, j: (0, 0)),
            pl.BlockSpec((None, k, tn), lambda i, j: (layer, 0, src_tile(j))),
            pl.BlockSpec((k, tn), lambda i, j: (0, 0)),
        ],
        out_specs=[
            pl.BlockSpec((tm, tn), lambda i, j: (i, jnp.minimum(j, n_nat - 1))),
            pl.BlockSpec((per_tile, groups, tm, dh), lambda i, j: (jnp.clip(j - n_nat, 0, n_head - 1), 0, i, 0)),
            pl.BlockSpec((tm // T_CHUNK, tn, T_CHUNK),
                         lambda i, j: (i, jnp.maximum(j - n_nat - n_head, 0), 0)),
        ],
        out_shape=[jax.ShapeDtypeStruct((m, n_nat * tn), BF16),
                   jax.ShapeDtypeStruct((n_head * per_tile, groups, m, dh), BF16),
                   jax.ShapeDtypeStruct((m // T_CHUNK, n_t * tn, T_CHUNK), BF16)],
        scratch_shapes=[pltpu.VMEM((tm, k), BF16), pltpu.VMEM((k, tn), BF16)],
        compiler_params=_params("parallel", "arbitrary"),
        name="in_projection",
    )(x, g.reshape(1, k), w, w_extra)


def _mlp_kernel(x_ref, g_ref, w1_ref, w2_ref, gf_ref, o_ref, xn_ref, h_ref, *, final_norm):
    f = pl.program_id(1)

    @pl.when(f == 0)
    def _():
        x = x_ref[...]
        xn_ref[...] = _rms(x, g_ref[...]).astype(BF16)
        o_ref[...] = x
        h_ref[...] = jnp.zeros(h_ref.shape, BF16)

    h_new = jnp.dot(xn_ref[...], w1_ref[...].astype(BF16), preferred_element_type=F32)
    o_ref[...] += jnp.dot(h_ref[...], w2_ref[...].astype(BF16), preferred_element_type=F32)
    h_ref[...] = jnp.square(jnp.maximum(h_new, 0.0)).astype(BF16)

    if final_norm:
        @pl.when(f == pl.num_programs(1) - 1)
        def _():
            o_ref[...] = _rms(o_ref[...], gf_ref[...])


def fused_mlp(x, g, w1, w2, layer, gf=None, *, tm=2048, tf=256):
    m, d = x.shape
    nf = w1.shape[2] // tf
    final_norm = gf is not None
    if gf is None:
        gf = g
    once = pl.Buffered(1)
    return pl.pallas_call(
        functools.partial(_mlp_kernel, final_norm=final_norm),
        grid=(m // tm, nf + 1),
        in_specs=[
            pl.BlockSpec((tm, d), lambda i, f: (i, 0), pipeline_mode=once),
            pl.BlockSpec((1, d), lambda i, f: (0, 0)),
            pl.BlockSpec((None, d, tf), lambda i, f: (layer, 0, jnp.minimum(f, nf - 1))),
            pl.BlockSpec((None, tf, d), lambda i, f: (layer, jnp.maximum(f - 1, 0), 0)),
            pl.BlockSpec((1, d), lambda i, f: (0, 0)),
        ],
        out_specs=pl.BlockSpec((tm, d), lambda i, f: (i, 0), pipeline_mode=once),
        out_shape=jax.ShapeDtypeStruct((m, d), F32),
        scratch_shapes=[pltpu.VMEM((tm, d), BF16), pltpu.VMEM((tm, tf), BF16)],
        compiler_params=_params("parallel", "arbitrary"),
        name="fused_mlp",
    )(x, g.reshape(1, d), w1, w2, gf.reshape(1, d))


def _conv_kernel(a_ref, gt_ref, w_ref, b_ref, lg_ref, lb_ref, o_ref, cs_ref, y_ref, wb_ref, *, tt):
    i = pl.program_id(0)

    @pl.when(i == 0)
    def _():
        cs_ref[0, 0:CONV_HALO, :] = jnp.zeros((CONV_HALO, cs_ref.shape[2]), F32)
        for k in range(CONV_WIDTH):
            wb_ref[k] = jnp.broadcast_to(w_ref[k:k + 1, :], wb_ref.shape[1:])

    @pl.when(i > 0)
    def _():
        cs_ref[0, 0:CONV_HALO, :] = cs_ref[0, tt:tt + CONV_HALO, :]

    cs_ref[0, CONV_HALO:, :] = a_ref[...].astype(F32) * jax.nn.sigmoid(gt_ref[...].astype(F32))
    c_ext = cs_ref[0]
    for b in range(1, SUBLANES):
        cs_ref[b] = pltpu.roll(c_ext, b, axis=0)

    rows = 4 * SUBLANES

    def chunk(j, carry):
        r0 = pl.multiple_of(j * rows, rows)
        groups = range(rows // SUBLANES)
        acc = [jnp.zeros((SUBLANES, cs_ref.shape[2]), F32) + b_ref[...] for _ in groups]
        for k in range(CONV_WIDTH):
            a, b = divmod(k, SUBLANES)
            wk = wb_ref[CONV_WIDTH - 1 - k]
            for q in groups:
                lo = pl.multiple_of(r0 + CONV_HALO + (q - a) * SUBLANES, SUBLANES)
                acc[q] = acc[q] + cs_ref[b, pl.ds(lo, SUBLANES), :] * wk
        for q in groups:
            y_ref[pl.ds(pl.multiple_of(r0 + q * SUBLANES, SUBLANES), SUBLANES), :] = acc[q]
        return carry

    lax.fori_loop(0, tt // rows, chunk, 0)

    acc = y_ref[...]
    mu = jnp.mean(acc, axis=-1, keepdims=True)
    xc = acc - mu
    var = jnp.mean(xc * xc, axis=-1, keepdims=True)
    y = xc * lax.rsqrt(var + EPS) * lg_ref[...] + lb_ref[...]
    o_ref[...] = (y * jax.nn.sigmoid(y)).astype(o_ref.dtype)


def conv_module(proj, conv_w, conv_b, ln_g, ln_b, *, tt=512):
    s = proj.shape[0]
    c = conv_w.shape[1]
    row = lambda v: v.reshape(1, c)
    return pl.pallas_call(
        functools.partial(_conv_kernel, tt=tt),
        grid=(s // tt,),
        in_specs=[
            pl.BlockSpec((tt, c), lambda i: (i, 0)),
            pl.BlockSpec((tt, c), lambda i: (i, 1)),
            pl.BlockSpec((CONV_WIDTH, c), lambda i: (0, 0)),
            pl.BlockSpec((1, c), lambda i: (0, 0)),
            pl.BlockSpec((1, c), lambda i: (0, 0)),
            pl.BlockSpec((1, c), lambda i: (0, 0)),
        ],
        out_specs=pl.BlockSpec((tt, c), lambda i: (i, 0)),
        out_shape=jax.ShapeDtypeStruct((s, c), BF16),
        scratch_shapes=[pltpu.VMEM((SUBLANES, tt + CONV_HALO, c), F32), pltpu.VMEM((tt, c), F32),
                        pltpu.VMEM((CONV_WIDTH, SUBLANES, c), F32)],
        compiler_params=_params("arbitrary"),
        name="conv_module",
    )(proj, proj, conv_w, row(conv_b), row(ln_g), row(ln_b))


def _compress_kernel(kt_ref, pos_ref, w1_ref, w2_ref, o_ref):
    kt = kt_ref[0, 0].astype(F32)
    half = kt.shape[1]
    first = (kt + pos_ref[0, 0:1, :]).astype(BF16)
    second = (kt + pos_ref[0, 1:2, :]).astype(BF16)
    p = jnp.dot(first, w1_ref[0, 0:half, :], preferred_element_type=F32)
    q = jnp.dot(second, w1_ref[0, half:, :], preferred_element_type=F32)
    n = q.shape[0]
    h = p + pltpu.roll(q, n - 1, axis=0)
    h = h * jax.nn.sigmoid(h)
    o_ref[0, 0] = jnp.dot(h.astype(BF16), w2_ref[0], preferred_element_type=F32).astype(o_ref.dtype)


def compress(kt, pos, w1, w2):
    two, g, nch, half = kt.shape
    hid = w1.shape[2]
    dh = w2.shape[2]
    return pl.pallas_call(
        _compress_kernel,
        grid=(two, g),
        in_specs=[
            pl.BlockSpec((1, 1, nch, half), lambda a, b: (a, b, 0, 0)),
            pl.BlockSpec((1, 2, half), lambda a, b: (a, 0, 0)),
            pl.BlockSpec((1, 2 * half, hid), lambda a, b: (a, 0, 0)),
            pl.BlockSpec((1, hid, dh), lambda a, b: (a, 0, 0)),
        ],
        out_specs=pl.BlockSpec((1, 1, nch, dh), lambda a, b: (a, b, 0, 0)),
        out_shape=jax.ShapeDtypeStruct((two, g, nch, dh), BF16),
        compiler_params=_params("parallel", "parallel"),
        name="compress",
    )(kt, pos, w1, w2)


def _split3(x):
    hi = x.astype(BF16)
    r1 = x - hi.astype(F32)
    mid = r1.astype(BF16)
    lo = (r1 - mid.astype(F32)).astype(BF16)
    return hi, mid, lo


def _nsa_kernel(qt_ref, kc_ref, vct_ref, oh_ref, ks_ref, vst_ref, kw_ref, vwt_ref, glt_ref, o_ref,
                qa_ref, m_ref, acc_ref, s_ref, part_ref, *, seq):
    i = pl.program_id(1)
    s0 = i * Q_BLOCK
    dh = NSA_HEAD_DIM
    cols = NSA_GROUP * Q_BLOCK
    pair = 2 * Q_BLOCK
    n_sel = seq // SEL_LEN
    n_cmp = kc_ref.shape[1]

    def with_ones(vt):
        return jnp.concatenate([vt, jnp.ones((ONES_ROWS, vt.shape[1]), BF16)], axis=0)

    qt = jnp.concatenate([qt_ref[0, r * dh:(r + 1) * dh, :] for r in range(NSA_GROUP)], axis=1)
    qt = (qt.astype(F32) * (dh ** -0.5 * LOG2E)).astype(BF16)
    t_col = s0 + (lax.broadcasted_iota(jnp.int32, (1, cols), 1) & (Q_BLOCK - 1))

    sc = jnp.dot(kc_ref[0], qt, preferred_element_type=F32)
    cmp_end = lax.broadcasted_iota(jnp.int32, (n_cmp, 1), 0) * CMP_STRIDE + (CMP_LEN - 1)
    sc = jnp.where(cmp_end <= t_col, sc, NEG_INF)
    e = jnp.exp2(sc - jnp.max(sc, axis=0, keepdims=True))
    l = jnp.sum(e, axis=0, keepdims=True)
    pc = e * jnp.where(t_col >= CMP_LEN - 1, 1.0 / l, 0.0)
    oc = jnp.dot(vct_ref[0], pc.astype(BF16), preferred_element_type=F32)

    imp_c = pc[:, 0:Q_BLOCK]
    for r in range(1, NSA_GROUP):
        imp_c = imp_c + pc[:, r * Q_BLOCK:(r + 1) * Q_BLOCK]
    sj = lax.broadcasted_iota(jnp.int32, (n_sel, n_cmp), 0) * SEL_LEN
    cn = lax.broadcasted_iota(jnp.int32, (n_sel, n_cmp), 1) * CMP_STRIDE
    overlap = jnp.where((cn <= sj + SEL_LEN - 1) & (cn + CMP_LEN - 1 >= sj), 1.0, 0.0).astype(BF16)
    imp = jnp.zeros((n_sel, Q_BLOCK), F32)
    for part in _split3(imp_c):
        imp = imp + jnp.dot(overlap, part, preferred_element_type=F32)

    t_win = t_col + (jnp.min(imp, keepdims=True) < 0.0).astype(jnp.int32)
    wlen = WINDOW + Q_BLOCK
    w0 = pl.multiple_of(jnp.maximum(s0 - WINDOW, 0), Q_BLOCK)
    wc = w0 >> (T_CHUNK.bit_length() - 1)
    sw = jnp.dot(kw_ref[0, pl.ds(w0, wlen), :], qt, preferred_element_type=F32)
    wpos = w0 + lax.broadcasted_iota(jnp.int32, (wlen, 1), 0)
    sw = jnp.where(wpos <= t_win, sw, NEG_INF)
    sw = jnp.concatenate([jnp.where(wpos[0:Q_BLOCK] > t_win - WINDOW, sw[0:Q_BLOCK], NEG_INF), sw[Q_BLOCK:]],
                         axis=0)
    pw = jnp.exp2((sw - jnp.max(sw, axis=0, keepdims=True)).astype(BF16))
    accw = jnp.zeros((dh + ONES_ROWS, cols), F32)
    for j in range(wlen // T_CHUNK):
        accw = accw + jnp.dot(with_ones(vwt_ref[wc + j]), pw[j * T_CHUNK:(j + 1) * T_CHUNK],
                              preferred_element_type=F32)
    ow = accw[0:dh] * (1.0 / accw[dh:dh + 1])

    def gate_row(b):
        gate = jax.nn.sigmoid(glt_ref[0].astype(F32))
        return jnp.concatenate(
            [gate[N_BRANCH * r + b:N_BRANCH * r + b + 1] for r in range(NSA_GROUP)], axis=1)

    part_ref[...] = gate_row(0) * oc + gate_row(2) * ow


    jj = lax.broadcasted_iota(jnp.int32, (n_sel, Q_BLOCK), 0)
    cur = (s0 + lax.broadcasted_iota(jnp.int32, (1, Q_BLOCK), 1)) >> (SEL_LEN.bit_length() - 1)
    forced = (jj == 0) | (jj == cur) | (jj == cur - 1)
    future = jj > cur
    assert FORCE_BONUS > 2 * NSA_GROUP
    n_forced = 3
    work = jnp.where(forced, -jnp.inf, jnp.where(future, NEG_INF, imp))
    jf = jj.astype(F32)
    chosen = jnp.where(forced, 1.0, 0.0)
    for _ in range(min(SEL_TOPK, n_sel) - n_forced):
        mx = jnp.max(work, axis=0, keepdims=True)
        first = jnp.min(jnp.where(work == mx, jf, float(n_sel)), axis=0, keepdims=True)
        hit = jf == first
        chosen = jnp.where(hit, 1.0, chosen)
        work = jnp.where(hit, -jnp.inf, work)
    bias = jnp.where((chosen > 0.0) & jnp.logical_not(future), 0.0, NEG_INF).astype(BF16)
    qa_ref[0:n_sel, :] = jnp.concatenate([bias] * NSA_GROUP, axis=1)
    qa_ref[n_sel:, :] = qt

    m_ref[...] = jnp.full(m_ref.shape, NEG_INF, F32)
    acc_ref[...] = jnp.zeros(acc_ref.shape, F32)

    def scores(c, dst_ref):
        k0 = pl.multiple_of(c * SEL_CHUNK, SEL_CHUNK)
        ka = jnp.concatenate([oh_ref[pl.ds(k0, SEL_CHUNK), :], ks_ref[0, pl.ds(k0, SEL_CHUNK), :]],
                             axis=1)
        for h0 in range(0, cols, pair):
            dst_ref[:, h0:h0 + pair] = jnp.dot(ka, qa_ref[:, h0:h0 + pair], preferred_element_type=F32)

    def accumulate(c, src_ref, causal, n_keys=SEL_CHUNK):
        s = src_ref[0:n_keys, :]
        if causal:
            kpos = c * SEL_CHUNK + lax.broadcasted_iota(jnp.int32, (n_keys, 1), 0)
            s = jnp.where(kpos <= t_col, s, NEG_INF)
        m_old = m_ref[...]
        m_new = jnp.maximum(m_old, jnp.max(s, axis=0, keepdims=True))
        p = jnp.exp2(s - m_new).astype(BF16)
        c0 = c * (SEL_CHUNK // T_CHUNK)
        vt = jnp.concatenate([vst_ref[c0 + k] for k in range(n_keys // T_CHUNK)], axis=1)
        pv = jnp.dot(with_ones(vt), p, preferred_element_type=F32)
        acc_ref[...] = acc_ref[...] * jnp.exp2(m_old - m_new) + pv
        m_ref[...] = m_new

    n_full = s0 >> (SEL_CHUNK.bit_length() - 1)
    sa_ref, sb_ref = s_ref.at[0], s_ref.at[1]
    scores(0, sa_ref)

    def body(j, carry):
        c = 2 * j
        scores(c + 1, sb_ref)
        accumulate(c, sa_ref, False)
        scores(c + 2, sa_ref)
        accumulate(c + 1, sb_ref, False)
        return carry

    lax.fori_loop(0, n_full >> 1, body, 0)

    @pl.when((n_full & 1) == 1)
    def _():
        scores(n_full, sb_ref)
        accumulate(n_full - 1, sa_ref, False)

    own = s0 - n_full * SEL_CHUNK
    for k in range(SEL_CHUNK // Q_BLOCK):
        @pl.when(own == k * Q_BLOCK)
        def _(k=k):
            accumulate(n_full, s_ref.at[n_full & 1], True, (k + 1) * Q_BLOCK)

    acc = acc_ref[...]
    os_ = acc[0:dh] * (1.0 / acc[dh:dh + 1])

    out_t = part_ref[...] + gate_row(1) * os_
    out_t = jnp.concatenate([out_t[:, r * Q_BLOCK:(r + 1) * Q_BLOCK] for r in range(NSA_GROUP)], axis=0)
    o_ref[...] = out_t.T.astype(o_ref.dtype)


def nsa_attention(proj_t, rows, kc, vct, onehot, ks, kw):
    nt = proj_t.shape[0]
    s = nt * T_CHUNK
    g, _, dh = kc.shape
    q_row, vs_row, vw_row, gate_row = rows
    assert Q_BLOCK == T_CHUNK and all(r % (NSA_GROUP * dh) == 0 for r in (q_row, vs_row, vw_row))
    assert gate_row % GATE_SLOT == 0
    dv = dh + ONES_ROWS
    cols = NSA_GROUP * Q_BLOCK
    whole = lambda a: pl.BlockSpec((1,) + a.shape[1:], lambda gi, i: (gi,) + (0,) * (a.ndim - 1))
    v_t = lambda row: pl.BlockSpec((nt, dh, T_CHUNK), lambda gi, i: (0, row // dh + gi, 0))
    return pl.pallas_call(
        functools.partial(_nsa_kernel, seq=s),
        grid=(g, s // Q_BLOCK),
        in_specs=[
            pl.BlockSpec((1, NSA_GROUP * dh, Q_BLOCK), lambda gi, i: (i, q_row // (NSA_GROUP * dh) + gi, 0)),
            whole(kc), whole(vct), pl.BlockSpec(onehot.shape, lambda gi, i: (0, 0)),
            whole(ks), v_t(vs_row), whole(kw), v_t(vw_row),
            pl.BlockSpec((1, GATE_SLOT, Q_BLOCK), lambda gi, i: (i, gate_row // GATE_SLOT + gi, 0)),
        ],
        out_specs=pl.BlockSpec((Q_BLOCK, NSA_GROUP * dh), lambda gi, i: (i, gi)),
        out_shape=jax.ShapeDtypeStruct((s, g * NSA_GROUP * dh), BF16),
        scratch_shapes=[pltpu.VMEM((onehot.shape[1] + dh, cols), BF16),
                        pltpu.VMEM((1, cols), F32),
                        pltpu.VMEM((dv, cols), F32),
                        pltpu.VMEM((2, SEL_CHUNK, cols), F32),
                        pltpu.VMEM((dh, cols), F32)],
        compiler_params=_params("parallel", "arbitrary"),
        name="nsa_attention",
    )(proj_t, kc, vct, onehot, ks, proj_t, kw, proj_t, proj_t)


def _out_proj_kernel(c_ref, o_ref, w_ref, x_ref, y_ref):
    half = c_ref.shape[1]
    acc = jnp.dot(c_ref[...], w_ref[0:half, :], preferred_element_type=F32)
    acc = acc + jnp.dot(o_ref[...], w_ref[half:, :], preferred_element_type=F32)
    y_ref[...] = x_ref[...] + acc


def out_proj(c, o, w, x, *, tm=1024, tn=512):
    m, half = c.shape
    n = w.shape[1]
    return pl.pallas_call(
        _out_proj_kernel,
        grid=(m // tm, n // tn),
        in_specs=[
            pl.BlockSpec((tm, half), lambda i, j: (i, 0)),
            pl.BlockSpec((tm, half), lambda i, j: (i, 0)),
            pl.BlockSpec((2 * half, tn), lambda i, j: (0, j)),
            pl.BlockSpec((tm, tn), lambda i, j: (i, j)),
        ],
        out_specs=pl.BlockSpec((tm, tn), lambda i, j: (i, j)),
        out_shape=jax.ShapeDtypeStruct((m, n), F32),
        compiler_params=_params("parallel", "parallel"),
        name="out_proj",
    )(c, o, w, x)


def _pool_kernel(x_ref, halo_ref, g_ref, w_ref, sc_ref, o_ref, *, tm):
    i = pl.program_id(0)
    x = x_ref[...]
    xn = _rms(x, g_ref[...])
    hn = _rms(halo_ref[...], g_ref[...]) * jnp.where(i > 0, 1.0, 0.0)
    ext = jnp.concatenate([hn, xn], axis=0)
    pd = w_ref.shape[1]
    pos1 = i * tm + 1 + lax.broadcasted_iota(jnp.int32, (tm, 1), 0)
    for gi, win in enumerate(POOL_WINDOWS):
        sl = slice(gi * pd, (gi + 1) * pd)
        s = ext[:, sl]
        sh = 1
        while sh < win:
            s = s + pltpu.roll(s, sh, axis=0)
            sh *= 2
        cnt = jnp.minimum(pos1, win).astype(F32)
        p = s[POOL_HALO:] / cnt - xn[:, sl]
        y = jnp.dot(p.astype(BF16), w_ref[gi], preferred_element_type=F32)
        o_ref[:, sl] = x[:, sl] + y * sc_ref[:, sl]


def pool_mixer(x, g, pool_w, pool_scale, *, tm=512):
    m, d = x.shape
    ng, pd, _ = pool_w.shape
    hb = tm // POOL_HALO
    return pl.pallas_call(
        functools.partial(_pool_kernel, tm=tm),
        grid=(m // tm,),
        in_specs=[
            pl.BlockSpec((tm, d), lambda i: (i, 0)),
            pl.BlockSpec((POOL_HALO, d), lambda i: (jnp.maximum(i * hb - 1, 0), 0)),
            pl.BlockSpec((1, d), lambda i: (0, 0)),
            pl.BlockSpec((ng, pd, pd), lambda i: (0, 0, 0)),
            pl.BlockSpec((1, d), lambda i: (0, 0)),
        ],
        out_specs=pl.BlockSpec((tm, d), lambda i: (i, 0)),
        out_shape=jax.ShapeDtypeStruct((m, d), F32),
        compiler_params=_params("parallel"),
        name="pool_mixer",
    )(x, x, g.reshape(1, d), pool_w, pool_scale.reshape(1, d))


def _even_mixer(x, g, w_in, li, w_out, conv_w, conv_b, ln_g, ln_b, pos_k, pos_v, kw1, kw2, vw1, vw2):
    s, d = x.shape
    dh, gk, hq = NSA_HEAD_DIM, NSA_KV_HEADS, NSA_HEADS
    conv_cols = 2 * conv_w.shape[1]
    q_cols, kv_cols = hq * dh, gk * dh
    tn = kv_cols
    q0 = conv_cols
    kv0 = q0 + q_cols
    assert q0 % tn == 0 and q_cols % tn == 0
    tiles = lambda c0, n: [(c0 + c) // tn for c in range(0, n, tn)]
    kv_tile = lambda k: (kv0 + k * kv_cols) // tn
    head = [kv_tile(0), kv_tile(1), kv_tile(2), kv_tile(4)]
    order = tiles(0, q0) + head + tiles(q0, q_cols) + [kv_tile(3), kv_tile(5)]
    w_gate = w_in[li, :, kv0 + 6 * kv_cols:].reshape(d, gk, NSA_GROUP * N_BRANCH)
    w_gate = jnp.pad(w_gate, ((0, 0), (0, 0), (0, GATE_SLOT - NSA_GROUP * N_BRANCH))).reshape(d, gk * GATE_SLOT)
    w_gate = jnp.pad(w_gate, ((0, 0), (0, tn - gk * GATE_SLOT))).astype(BF16)

    proj, proj_h, proj_t = in_projection(x, g, w_in, li, order, w_gate, q0 // tn, len(head), gk, dh)
    c = conv_module(proj, conv_w, conv_b, ln_g, ln_b)

    k_s, k_w = proj_h[2], proj_h[3]
    nch = s // CMP_STRIDE
    kt = proj_h[0:2].reshape(2, gk, nch, CMP_STRIDE * dh)
    pos = jnp.stack([pos_k, pos_v]).reshape(2, 2, CMP_STRIDE * dh)
    kvc = compress(kt, pos, jnp.stack([kw1, vw1]).astype(BF16), jnp.stack([kw2, vw2]).astype(BF16))

    n_sel = s // SEL_LEN
    onehot = (jnp.arange(s)[:, None] // SEL_LEN == jnp.arange(n_sel)[None, :]).astype(BF16)
    rows = (0, q_cols, q_cols + kv_cols, q_cols + 2 * kv_cols)
    o = nsa_attention(proj_t, rows, kvc[0], kvc[1].transpose(0, 2, 1), onehot, k_s, k_w)
    return out_proj(c, o, w_out.astype(BF16), x)


def kernel(x, mix_norm, mlp_norm, w_mlp_in, w_mlp_out, w_in, w_out, conv_w, conv_b, conv_ln_g,
           conv_ln_b, cmp_pos_k, cmp_pos_v, cmp_k_w1, cmp_k_w2, cmp_v_w1, cmp_v_w2, pool_w,
           pool_scale, final_norm):
    b, s, d = x.shape
    assert b == 1
    depth = mix_norm.shape[0]
    h = x.reshape(s, d)
    for layer in range(depth):
        i = layer // 2
        if layer % 2 == 0:
            h = _even_mixer(h, mix_norm[layer], w_in, i, w_out[i], conv_w[i], conv_b[i],
                            conv_ln_g[i], conv_ln_b[i], cmp_pos_k[i], cmp_pos_v[i],
                            cmp_k_w1[i], cmp_k_w2[i], cmp_v_w1[i], cmp_v_w2[i])
        else:
            h = pool_mixer(h, mix_norm[layer], pool_w[i].astype(BF16), pool_scale[i])
        fg = final_norm if layer == depth - 1 else None
        h = fused_mlp(h, mlp_norm[layer], w_mlp_in, w_mlp_out, layer, fg)
    return h.reshape(b, s, d)
```

```python
import functools

import jax
import jax.numpy as jnp
from jax import lax
from jax.experimental import pallas as pl
from jax.experimental.pallas import tpu as pltpu

F32 = jnp.float32
BF16 = jnp.bfloat16

EPS = 1e-6
NEG_INF = -1e30
LOG2E = 1.4426950408889634
CONV_WIDTH = 31
SUBLANES = 8
CONV_HALO = 32
NSA_HEADS = 16
NSA_HEAD_DIM = 64
NSA_KV_HEADS = 4
NSA_GROUP = NSA_HEADS // NSA_KV_HEADS
N_BRANCH = 3
CMP_LEN = 32
CMP_STRIDE = 16
SEL_LEN = 64
SEL_TOPK = 16
WINDOW = 512
Q_BLOCK = 256
FORCE_BONUS = 1e3
POOL_WINDOWS = (2, 4, 8, 16)
POOL_HALO = 16
SEL_CHUNK = 512
ONES_ROWS = 16
T_CHUNK = 256
GATE_SLOT = 16

VMEM_LIMIT = 56 * 1024 * 1024


def _params(*sem):
    return pltpu.CompilerParams(dimension_semantics=sem, vmem_limit_bytes=VMEM_LIMIT)


def _rms(x, g):
    r = lax.rsqrt(jnp.mean(x * x, axis=-1, keepdims=True) + EPS)
    return x * r * g


def _in_proj_kernel(x_ref, g_ref, w_ref, on_ref, oh_ref, ot_ref, xn_ref, *, n_nat, n_head):
    j = pl.program_id(1)

    @pl.when(j == 0)
    def _():
        xn_ref[...] = _rms(x_ref[...], g_ref[...]).astype(BF16)

    acc = jnp.dot(xn_ref[...], w_ref[...], preferred_element_type=F32)

    @pl.when(j < n_nat)
    def _():
        on_ref[...] = acc.astype(on_ref.dtype)

    @pl.when((j >= n_nat) & (j < n_nat + n_head))
    def _():
        dh = oh_ref.shape[3]
        for t in range(oh_ref.shape[0]):
            for gi in range(oh_ref.shape[1]):
                c0 = (t * oh_ref.shape[1] + gi) * dh
                oh_ref[t, gi] = acc[:, c0:c0 + dh].astype(oh_ref.dtype)

    @pl.when(j >= n_nat + n_head)
    def _():
        acc_t = acc.T
        for k in range(ot_ref.shape[0]):
            ot_ref[k] = acc_t[:, k * T_CHUNK:(k + 1) * T_CHUNK].astype(ot_ref.dtype)


def in_projection(x, g, w, n_nat, n_head, groups, dh, *, tm=1024, tn=512):
    m, k = x.shape
    n_t = w.shape[1] // tn - n_nat - n_head
    per_tile = tn // (groups * dh)
    assert w.shape[1] % tn == 0 and m % tm == 0 and tm % T_CHUNK == 0 and tn % (groups * dh) == 0
    return pl.pallas_call(
        functools.partial(_in_proj_kernel, n_nat=n_nat, n_head=n_head),
        grid=(m // tm, n_nat + n_head + n_t),
        in_specs=[
            pl.BlockSpec((tm, k), lambda i, j: (i, 0)),
            pl.BlockSpec((1, k), lambda i, j: (0, 0)),
            pl.BlockSpec((k, tn), lambda i, j: (0, j)),
        ],
        out_specs=[
            pl.BlockSpec((tm, tn), lambda i, j: (i, jnp.minimum(j, n_nat - 1))),
            pl.BlockSpec((per_tile, groups, tm, dh), lambda i, j: (jnp.clip(j - n_nat, 0, n_head - 1), 0, i, 0)),
            pl.BlockSpec((tm // T_CHUNK, tn, T_CHUNK),
                         lambda i, j: (i, jnp.maximum(j - n_nat - n_head, 0), 0)),
        ],
        out_shape=[jax.ShapeDtypeStruct((m, n_nat * tn), BF16),
                   jax.ShapeDtypeStruct((n_head * per_tile, groups, m, dh), BF16),
                   jax.ShapeDtypeStruct((m // T_CHUNK, n_t * tn, T_CHUNK), BF16)],
        scratch_shapes=[pltpu.VMEM((tm, k), BF16)],
        compiler_params=_params("parallel", "arbitrary"),
        name="in_projection",
    )(x, g.reshape(1, k), w)


def _mlp_kernel(x_ref, g_ref, w1_ref, w2_ref, gf_ref, o_ref, xn_ref, h_ref, *, final_norm):
    f = pl.program_id(1)

    @pl.when(f == 0)
    def _():
        x = x_ref[...]
        xn_ref[...] = _rms(x, g_ref[...]).astype(BF16)
        o_ref[...] = x
        h_ref[...] = jnp.zeros(h_ref.shape, BF16)

    h_new = jnp.dot(xn_ref[...], w1_ref[...].astype(BF16), preferred_element_type=F32)
    o_ref[...] += jnp.dot(h_ref[...], w2_ref[...].astype(BF16), preferred_element_type=F32)
    h_ref[...] = jnp.square(jnp.maximum(h_new, 0.0)).astype(BF16)

    if final_norm:
        @pl.when(f == pl.num_programs(1) - 1)
        def _():
            o_ref[...] = _rms(o_ref[...], gf_ref[...])


def fused_mlp(x, g, w1, w2, layer, gf=None, *, tm=2048, tf=256):
    m, d = x.shape
    nf = w1.shape[2] // tf
    final_norm = gf is not None
    if gf is None:
        gf = g
    once = pl.Buffered(1)
    return pl.pallas_call(
        functools.partial(_mlp_kernel, final_norm=final_norm),
        grid=(m // tm, nf + 1),
        in_specs=[
            pl.BlockSpec((tm, d), lambda i, f: (i, 0), pipeline_mode=once),
            pl.BlockSpec((1, d), lambda i, f: (0, 0)),
            pl.BlockSpec((None, d, tf), lambda i, f: (layer, 0, jnp.minimum(f, nf - 1))),
            pl.BlockSpec((None, tf, d), lambda i, f: (layer, jnp.maximum(f - 1, 0), 0)),
            pl.BlockSpec((1, d), lambda i, f: (0, 0)),
        ],
        out_specs=pl.BlockSpec((tm, d), lambda i, f: (i, 0), pipeline_mode=once),
        out_shape=jax.ShapeDtypeStruct((m, d), F32),
        scratch_shapes=[pltpu.VMEM((tm, d), BF16), pltpu.VMEM((tm, tf), BF16)],
        compiler_params=_params("parallel", "arbitrary"),
        name="fused_mlp",
    )(x, g.reshape(1, d), w1, w2, gf.reshape(1, d))


def _conv_kernel(a_ref, gt_ref, w_ref, b_ref, lg_ref, lb_ref, o_ref, cs_ref, y_ref, wb_ref, *, tt):
    i = pl.program_id(0)

    @pl.when(i == 0)
    def _():
        cs_ref[0, 0:CONV_HALO, :] = jnp.zeros((CONV_HALO, cs_ref.shape[2]), F32)
        for k in range(CONV_WIDTH):
            wb_ref[k] = jnp.broadcast_to(w_ref[k:k + 1, :], wb_ref.shape[1:])

    @pl.when(i > 0)
    def _():
        cs_ref[0, 0:CONV_HALO, :] = cs_ref[0, tt:tt + CONV_HALO, :]

    cs_ref[0, CONV_HALO:, :] = a_ref[...].astype(F32) * jax.nn.sigmoid(gt_ref[...].astype(F32))
    c_ext = cs_ref[0]
    for b in range(1, SUBLANES):
        cs_ref[b] = pltpu.roll(c_ext, b, axis=0)

    rows = 4 * SUBLANES

    def chunk(j, carry):
        r0 = pl.multiple_of(j * rows, rows)
        groups = range(rows // SUBLANES)
        acc = [jnp.zeros((SUBLANES, cs_ref.shape[2]), F32) + b_ref[...] for _ in groups]
        for k in range(CONV_WIDTH):
            a, b = divmod(k, SUBLANES)
            wk = wb_ref[CONV_WIDTH - 1 - k]
            for q in groups:
                lo = pl.multiple_of(r0 + CONV_HALO + (q - a) * SUBLANES, SUBLANES)
                acc[q] = acc[q] + cs_ref[b, pl.ds(lo, SUBLANES), :] * wk
        for q in groups:
            y_ref[pl.ds(pl.multiple_of(r0 + q * SUBLANES, SUBLANES), SUBLANES), :] = acc[q]
        return carry

    lax.fori_loop(0, tt // rows, chunk, 0)

    acc = y_ref[...]
    mu = jnp.mean(acc, axis=-1, keepdims=True)
    xc = acc - mu
    var = jnp.mean(xc * xc, axis=-1, keepdims=True)
    y = xc * lax.rsqrt(var + EPS) * lg_ref[...] + lb_ref[...]
    o_ref[...] = (y * jax.nn.sigmoid(y)).astype(o_ref.dtype)


def conv_module(proj, conv_w, conv_b, ln_g, ln_b, *, tt=512):
    s = proj.shape[0]
    c = conv_w.shape[1]
    row = lambda v: v.reshape(1, c)
    return pl.pallas_call(
        functools.partial(_conv_kernel, tt=tt),
        grid=(s // tt,),
        in_specs=[
            pl.BlockSpec((tt, c), lambda i: (i, 0)),
            pl.BlockSpec((tt, c), lambda i: (i, 1)),
            pl.BlockSpec((CONV_WIDTH, c), lambda i: (0, 0)),
            pl.BlockSpec((1, c), lambda i: (0, 0)),
            pl.BlockSpec((1, c), lambda i: (0, 0)),
            pl.BlockSpec((1, c), lambda i: (0, 0)),
        ],
        out_specs=pl.BlockSpec((tt, c), lambda i: (i, 0)),
        out_shape=jax.ShapeDtypeStruct((s, c), BF16),
        scratch_shapes=[pltpu.VMEM((SUBLANES, tt + CONV_HALO, c), F32), pltpu.VMEM((tt, c), F32),
                        pltpu.VMEM((CONV_WIDTH, SUBLANES, c), F32)],
        compiler_params=_params("arbitrary"),
        name="conv_module",
    )(proj, proj, conv_w, row(conv_b), row(ln_g), row(ln_b))


def _compress_kernel(kt_ref, pos_ref, w1_ref, w2_ref, o_ref):
    kt = kt_ref[0, 0].astype(F32)
    half = kt.shape[1]
    first = (kt + pos_ref[0, 0:1, :]).astype(BF16)
    second = (kt + pos_ref[0, 1:2, :]).astype(BF16)
    p = jnp.dot(first, w1_ref[0, 0:half, :], preferred_element_type=F32)
    q = jnp.dot(second, w1_ref[0, half:, :], preferred_element_type=F32)
    n = q.shape[0]
    h = p + pltpu.roll(q, n - 1, axis=0)
    h = h * jax.nn.sigmoid(h)
    o_ref[0, 0] = jnp.dot(h.astype(BF16), w2_ref[0], preferred_element_type=F32).astype(o_ref.dtype)


def compress(kt, pos, w1, w2):
    two, g, nch, half = kt.shape
    hid = w1.shape[2]
    dh = w2.shape[2]
    return pl.pallas_call(
        _compress_kernel,
        grid=(two, g),
        in_specs=[
            pl.BlockSpec((1, 1, nch, half), lambda a, b: (a, b, 0, 0)),
            pl.BlockSpec((1, 2, half), lambda a, b: (a, 0, 0)),
            pl.BlockSpec((1, 2 * half, hid), lambda a, b: (a, 0, 0)),
            pl.BlockSpec((1, hid, dh), lambda a, b: (a, 0, 0)),
        ],
        out_specs=pl.BlockSpec((1, 1, nch, dh), lambda a, b: (a, b, 0, 0)),
        out_shape=jax.ShapeDtypeStruct((two, g, nch, dh), BF16),
        compiler_params=_params("parallel", "parallel"),
        name="compress",
    )(kt, pos, w1, w2)


def _split3(x):
    hi = x.astype(BF16)
    r1 = x - hi.astype(F32)
    mid = r1.astype(BF16)
    lo = (r1 - mid.astype(F32)).astype(BF16)
    return hi, mid, lo


def _nsa_kernel(qt_ref, kc_ref, vct_ref, oh_ref, ks_ref, vst_ref, kw_ref, vwt_ref, glt_ref, o_ref,
                qa_ref, m_ref, acc_ref, s_ref, part_ref, *, seq):
    i = pl.program_id(1)
    s0 = i * Q_BLOCK
    dh = NSA_HEAD_DIM
    cols = NSA_GROUP * Q_BLOCK
    pair = 2 * Q_BLOCK
    n_sel = seq // SEL_LEN
    n_cmp = kc_ref.shape[1]

    def with_ones(vt):
        return jnp.concatenate([vt, jnp.ones((ONES_ROWS, vt.shape[1]), BF16)], axis=0)

    qt = jnp.concatenate([qt_ref[0, r * dh:(r + 1) * dh, :] for r in range(NSA_GROUP)], axis=1)
    qt = (qt.astype(F32) * (dh ** -0.5 * LOG2E)).astype(BF16)
    t_col = s0 + (lax.broadcasted_iota(jnp.int32, (1, cols), 1) & (Q_BLOCK - 1))

    sc = jnp.dot(kc_ref[0], qt, preferred_element_type=F32)
    cmp_end = lax.broadcasted_iota(jnp.int32, (n_cmp, 1), 0) * CMP_STRIDE + (CMP_LEN - 1)
    sc = jnp.where(cmp_end <= t_col, sc, NEG_INF)
    e = jnp.exp2(sc - jnp.max(sc, axis=0, keepdims=True))
    l = jnp.sum(e, axis=0, keepdims=True)
    pc = e * jnp.where(t_col >= CMP_LEN - 1, 1.0 / l, 0.0)
    oc = jnp.dot(vct_ref[0], pc.astype(BF16), preferred_element_type=F32)

    imp_c = pc[:, 0:Q_BLOCK]
    for r in range(1, NSA_GROUP):
        imp_c = imp_c + pc[:, r * Q_BLOCK:(r + 1) * Q_BLOCK]
    sj = lax.broadcasted_iota(jnp.int32, (n_sel, n_cmp), 0) * SEL_LEN
    cn = lax.broadcasted_iota(jnp.int32, (n_sel, n_cmp), 1) * CMP_STRIDE
    overlap = jnp.where((cn <= sj + SEL_LEN - 1) & (cn + CMP_LEN - 1 >= sj), 1.0, 0.0).astype(BF16)
    imp = jnp.zeros((n_sel, Q_BLOCK), F32)
    for part in _split3(imp_c):
        imp = imp + jnp.dot(overlap, part, preferred_element_type=F32)

    t_win = t_col + (jnp.min(imp, keepdims=True) < 0.0).astype(jnp.int32)
    wlen = WINDOW + Q_BLOCK
    w0 = pl.multiple_of(jnp.maximum(s0 - WINDOW, 0), Q_BLOCK)
    wc = w0 >> (T_CHUNK.bit_length() - 1)
    sw = jnp.dot(kw_ref[0, pl.ds(w0, wlen), :], qt, preferred_element_type=F32)
    wpos = w0 + lax.broadcasted_iota(jnp.int32, (wlen, 1), 0)
    sw = jnp.where(wpos <= t_win, sw, NEG_INF)
    sw = jnp.concatenate([jnp.where(wpos[0:Q_BLOCK] > t_win - WINDOW, sw[0:Q_BLOCK], NEG_INF), sw[Q_BLOCK:]],
                         axis=0)
    pw = jnp.exp2((sw - jnp.max(sw, axis=0, keepdims=True)).astype(BF16))
    accw = jnp.zeros((dh + ONES_ROWS, cols), F32)
    for j in range(wlen // T_CHUNK):
        accw = accw + jnp.dot(with_ones(vwt_ref[wc + j]), pw[j * T_CHUNK:(j + 1) * T_CHUNK],
                              preferred_element_type=F32)
    ow = accw[0:dh] * (1.0 / accw[dh:dh + 1])

    def gate_row(b):
        gate = jax.nn.sigmoid(glt_ref[0].astype(F32))
        return jnp.concatenate(
            [gate[N_BRANCH * r + b:N_BRANCH * r + b + 1] for r in range(NSA_GROUP)], axis=1)

    part_ref[...] = gate_row(0) * oc + gate_row(2) * ow


    jj = lax.broadcasted_iota(jnp.int32, (n_sel, Q_BLOCK), 0)
    cur = (s0 + lax.broadcasted_iota(jnp.int32, (1, Q_BLOCK), 1)) >> (SEL_LEN.bit_length() - 1)
    forced = (jj == 0) | (jj == cur) | (jj == cur - 1)
    future = jj > cur
    assert FORCE_BONUS > 2 * NSA_GROUP
    n_forced = 3
    work = jnp.where(forced, -jnp.inf, jnp.where(future, NEG_INF, imp))
    jf = jj.astype(F32)
    chosen = jnp.where(forced, 1.0, 0.0)
    for _ in range(min(SEL_TOPK, n_sel) - n_forced):
        mx = jnp.max(work, axis=0, keepdims=True)
        first = jnp.min(jnp.where(work == mx, jf, float(n_sel)), axis=0, keepdims=True)
        hit = jf == first
        chosen = jnp.where(hit, 1.0, chosen)
        work = jnp.where(hit, -jnp.inf, work)
    bias = jnp.where((chosen > 0.0) & jnp.logical_not(future), 0.0, NEG_INF).astype(BF16)
    qa_ref[0:n_sel, :] = jnp.concatenate([bias] * NSA_GROUP, axis=1)
    qa_ref[n_sel:, :] = qt

    m_ref[...] = jnp.full(m_ref.shape, NEG_INF, F32)
    acc_ref[...] = jnp.zeros(acc_ref.shape, F32)

    def scores(c, dst_ref):
        k0 = pl.multiple_of(c * SEL_CHUNK, SEL_CHUNK)
        ka = jnp.concatenate([oh_ref[pl.ds(k0, SEL_CHUNK), :], ks_ref[0, pl.ds(k0, SEL_CHUNK), :]],
                             axis=1)
        for h0 in range(0, cols, pair):
            dst_ref[:, h0:h0 + pair] = jnp.dot(ka, qa_ref[:, h0:h0 + pair], preferred_element_type=F32)

    def accumulate(c, src_ref, causal, n_keys=SEL_CHUNK):
        s = src_ref[0:n_keys, :]
        if causal:
            kpos = c * SEL_CHUNK + lax.broadcasted_iota(jnp.int32, (n_keys, 1), 0)
            s = jnp.where(kpos <= t_col, s, NEG_INF)
        m_old = m_ref[...]
        m_new = jnp.maximum(m_old, jnp.max(s, axis=0, keepdims=True))
        p = jnp.exp2(s - m_new).astype(BF16)
        c0 = c * (SEL_CHUNK // T_CHUNK)
        vt = jnp.concatenate([vst_ref[c0 + k] for k in range(n_keys // T_CHUNK)], axis=1)
        pv = jnp.dot(with_ones(vt), p, preferred_element_type=F32)
        acc_ref[...] = acc_ref[...] * jnp.exp2(m_old - m_new) + pv
        m_ref[...] = m_new

    n_full = s0 >> (SEL_CHUNK.bit_length() - 1)
    sa_ref, sb_ref = s_ref.at[0], s_ref.at[1]
    scores(0, sa_ref)

    def body(j, carry):
        c = 2 * j
        scores(c + 1, sb_ref)
        accumulate(c, sa_ref, False)
        scores(c + 2, sa_ref)
        accumulate(c + 1, sb_ref, False)
        return carry

    lax.fori_loop(0, n_full >> 1, body, 0)

    @pl.when((n_full & 1) == 1)
    def _():
        scores(n_full, sb_ref)
        accumulate(n_full - 1, sa_ref, False)

    own = s0 - n_full * SEL_CHUNK
    for k in range(SEL_CHUNK // Q_BLOCK):
        @pl.when(own == k * Q_BLOCK)
        def _(k=k):
            accumulate(n_full, s_ref.at[n_full & 1], True, (k + 1) * Q_BLOCK)

    acc = acc_ref[...]
    os_ = acc[0:dh] * (1.0 / acc[dh:dh + 1])

    out_t = part_ref[...] + gate_row(1) * os_
    out_t = jnp.concatenate([out_t[:, r * Q_BLOCK:(r + 1) * Q_BLOCK] for r in range(NSA_GROUP)], axis=0)
    o_ref[...] = out_t.T.astype(o_ref.dtype)


def nsa_attention(proj_t, rows, kc, vct, onehot, proj_h, ks_id, kw_id):
    nt = proj_t.shape[0]
    s = nt * T_CHUNK
    g, _, dh = kc.shape
    q_row, vs_row, vw_row, gate_row = rows
    assert Q_BLOCK == T_CHUNK and all(r % (NSA_GROUP * dh) == 0 for r in (q_row, vs_row, vw_row))
    assert gate_row % GATE_SLOT == 0
    dv = dh + ONES_ROWS
    cols = NSA_GROUP * Q_BLOCK
    whole = lambda a: pl.BlockSpec((1,) + a.shape[1:], lambda gi, i: (gi,) + (0,) * (a.ndim - 1))
    v_t = lambda row: pl.BlockSpec((nt, dh, T_CHUNK), lambda gi, i: (0, row // dh + gi, 0))
    k_h = lambda t: pl.BlockSpec((None, 1, s, dh), lambda gi, i: (t, gi, 0, 0))
    return pl.pallas_call(
        functools.partial(_nsa_kernel, seq=s),
        grid=(g, s // Q_BLOCK),
        in_specs=[
            pl.BlockSpec((1, NSA_GROUP * dh, Q_BLOCK), lambda gi, i: (i, q_row // (NSA_GROUP * dh) + gi, 0)),
            whole(kc), whole(vct), pl.BlockSpec(onehot.shape, lambda gi, i: (0, 0)),
            k_h(ks_id), v_t(vs_row), k_h(kw_id), v_t(vw_row),
            pl.BlockSpec((1, GATE_SLOT, Q_BLOCK), lambda gi, i: (i, gate_row // GATE_SLOT + gi, 0)),
        ],
        out_specs=pl.BlockSpec((Q_BLOCK, NSA_GROUP * dh), lambda gi, i: (i, gi)),
        out_shape=jax.ShapeDtypeStruct((s, g * NSA_GROUP * dh), BF16),
        scratch_shapes=[pltpu.VMEM((onehot.shape[1] + dh, cols), BF16),
                        pltpu.VMEM((1, cols), F32),
                        pltpu.VMEM((dv, cols), F32),
                        pltpu.VMEM((2, SEL_CHUNK, cols), F32),
                        pltpu.VMEM((dh, cols), F32)],
        compiler_params=_params("parallel", "arbitrary"),
        name="nsa_attention",
    )(proj_t, kc, vct, onehot, proj_h, proj_t, proj_h, proj_t, proj_t)


def _out_proj_kernel(c_ref, o_ref, w_ref, x_ref, y_ref):
    half = c_ref.shape[1]
    acc = jnp.dot(c_ref[...], w_ref[0:half, :], preferred_element_type=F32)
    acc = acc + jnp.dot(o_ref[...], w_ref[half:, :], preferred_element_type=F32)
    y_ref[...] = x_ref[...] + acc


def out_proj(c, o, w, x, *, tm=1024, tn=512):
    m, half = c.shape
    n = w.shape[1]
    return pl.pallas_call(
        _out_proj_kernel,
        grid=(m // tm, n // tn),
        in_specs=[
            pl.BlockSpec((tm, half), lambda i, j: (i, 0)),
            pl.BlockSpec((tm, half), lambda i, j: (i, 0)),
            pl.BlockSpec((2 * half, tn), lambda i, j: (0, j)),
            pl.BlockSpec((tm, tn), lambda i, j: (i, j)),
        ],
        out_specs=pl.BlockSpec((tm, tn), lambda i, j: (i, j)),
        out_shape=jax.ShapeDtypeStruct((m, n), F32),
        compiler_params=_params("parallel", "parallel"),
        name="out_proj",
    )(c, o, w, x)


def _pool_kernel(x_ref, halo_ref, g_ref, w_ref, sc_ref, o_ref, *, tm):
    i = pl.program_id(0)
    x = x_ref[...]
    xn = _rms(x, g_ref[...])
    hn = _rms(halo_ref[...], g_ref[...]) * jnp.where(i > 0, 1.0, 0.0)
    ext = jnp.concatenate([hn, xn], axis=0)
    pd = w_ref.shape[1]
    pos1 = i * tm + 1 + lax.broadcasted_iota(jnp.int32, (tm, 1), 0)
    for gi, win in enumerate(POOL_WINDOWS):
        sl = slice(gi * pd, (gi + 1) * pd)
        s = ext[:, sl]
        sh = 1
        while sh < win:
            s = s + pltpu.roll(s, sh, axis=0)
            sh *= 2
        cnt = jnp.minimum(pos1, win).astype(F32)
        p = s[POOL_HALO:] / cnt - xn[:, sl]
        y = jnp.dot(p.astype(BF16), w_ref[gi], preferred_element_type=F32)
        o_ref[:, sl] = x[:, sl] + y * sc_ref[:, sl]


def pool_mixer(x, g, pool_w, pool_scale, *, tm=512):
    m, d = x.shape
    ng, pd, _ = pool_w.shape
    hb = tm // POOL_HALO
    return pl.pallas_call(
        functools.partial(_pool_kernel, tm=tm),
        grid=(m // tm,),
        in_specs=[
            pl.BlockSpec((tm, d), lambda i: (i, 0)),
            pl.BlockSpec((POOL_HALO, d), lambda i: (jnp.maximum(i * hb - 1, 0), 0)),
            pl.BlockSpec((1, d), lambda i: (0, 0)),
            pl.BlockSpec((ng, pd, pd), lambda i: (0, 0, 0)),
            pl.BlockSpec((1, d), lambda i: (0, 0)),
        ],
        out_specs=pl.BlockSpec((tm, d), lambda i: (i, 0)),
        out_shape=jax.ShapeDtypeStruct((m, d), F32),
        compiler_params=_params("parallel"),
        name="pool_mixer",
    )(x, x, g.reshape(1, d), pool_w, pool_scale.reshape(1, d))


def _even_mixer(x, g, w_in, w_out, conv_w, conv_b, ln_g, ln_b, pos_k, pos_v, kw1, kw2, vw1, vw2):
    s, d = x.shape
    dh, gk, hq = NSA_HEAD_DIM, NSA_KV_HEADS, NSA_HEADS
    conv_cols = 2 * conv_w.shape[1]
    q_cols, kv_cols = hq * dh, gk * dh
    tn = 512
    q0 = conv_cols
    kv = lambda k: w_in[:, q0 + q_cols + k * kv_cols:q0 + q_cols + (k + 1) * kv_cols]
    w_gate = w_in[:, q0 + q_cols + 6 * kv_cols:].reshape(d, gk, NSA_GROUP * N_BRANCH)
    w_gate = jnp.pad(w_gate, ((0, 0), (0, 0), (0, GATE_SLOT - NSA_GROUP * N_BRANCH))).reshape(d, gk * GATE_SLOT)
    head = [kv(0), kv(1), kv(2), kv(4)]
    feat = [w_in[:, q0:q0 + q_cols], kv(3), kv(5), w_gate]
    n_feat_cols = sum(w.shape[1] for w in feat)
    assert conv_cols % tn == 0 and (len(head) * kv_cols) % tn == 0
    w_all = jnp.concatenate([w_in[:, :q0]] + head + feat + [jnp.zeros((d, (-n_feat_cols) % tn), F32)],
                            axis=1).astype(BF16)

    proj, proj_h, proj_t = in_projection(x, g, w_all, conv_cols // tn, len(head) * kv_cols // tn, gk, dh, tn=tn)
    c = conv_module(proj, conv_w, conv_b, ln_g, ln_b)

    nch = s // CMP_STRIDE
    kt = proj_h[0:2].reshape(2, gk, nch, CMP_STRIDE * dh)
    pos = jnp.stack([pos_k, pos_v]).reshape(2, 2, CMP_STRIDE * dh)
    kvc = compress(kt, pos, jnp.stack([kw1, vw1]).astype(BF16), jnp.stack([kw2, vw2]).astype(BF16))

    n_sel = s // SEL_LEN
    onehot = (jnp.arange(s)[:, None] // SEL_LEN == jnp.arange(n_sel)[None, :]).astype(BF16)
    rows = (0, q_cols, q_cols + kv_cols, q_cols + 2 * kv_cols)
    o = nsa_attention(proj_t, rows, kvc[0], kvc[1].transpose(0, 2, 1), onehot, proj_h, 2, 3)
    return out_proj(c, o, w_out.astype(BF16), x)


def kernel(x, mix_norm, mlp_norm, w_mlp_in, w_mlp_out, w_in, w_out, conv_w, conv_b, conv_ln_g,
           conv_ln_b, cmp_pos_k, cmp_pos_v, cmp_k_w1, cmp_k_w2, cmp_v_w1, cmp_v_w2, pool_w,
           pool_scale, final_norm):
    b, s, d = x.shape
    assert b == 1
    depth = mix_norm.shape[0]
    h = x.reshape(s, d)
    for layer in range(depth):
        i = layer // 2
        if layer % 2 == 0:
            h = _even_mixer(h, mix_norm[layer], w_in[i], w_out[i], conv_w[i], conv_b[i],
                            conv_ln_g[i], conv_ln_b[i], cmp_pos_k[i], cmp_pos_v[i],
                            cmp_k_w1[i], cmp_k_w2[i], cmp_v_w1[i], cmp_v_w2[i])
        else:
            h = pool_mixer(h, mix_norm[layer], pool_w[i].astype(BF16), pool_scale[i])
        fg = final_norm if layer == depth - 1 else None
        h = fused_mlp(h, mlp_norm[layer], w_mlp_in, w_mlp_out, layer, fg)
    return h.reshape(b, s, d)
```

```python
import functools

import jax
import jax.numpy as jnp
from jax import lax
from jax.experimental import pallas as pl
from jax.experimental.pallas import tpu as pltpu

F32 = jnp.float32
BF16 = jnp.bfloat16

EPS = 1e-6
NEG_INF = -1e30
LOG2E = 1.4426950408889634
CONV_WIDTH = 31
SUBLANES = 8
CONV_HALO = 32
NSA_HEADS = 16
NSA_HEAD_DIM = 64
NSA_KV_HEADS = 4
NSA_GROUP = NSA_HEADS // NSA_KV_HEADS
N_BRANCH = 3
CMP_LEN = 32
CMP_STRIDE = 16
SEL_LEN = 64
SEL_TOPK = 16
WINDOW = 512
Q_BLOCK = 256
FORCE_BONUS = 1e3
POOL_WINDOWS = (2, 4, 8, 16)
POOL_HALO = 16
SEL_CHUNK = 512
ONES_ROWS = 16
T_CHUNK = 256
GATE_SLOT = 16

VMEM_LIMIT = 56 * 1024 * 1024


def _params(*sem):
    return pltpu.CompilerParams(dimension_semantics=sem, vmem_limit_bytes=VMEM_LIMIT)


def _rms(x, g):
    r = lax.rsqrt(jnp.mean(x * x, axis=-1, keepdims=True) + EPS)
    return x * r * g


def _in_proj_kernel(x_ref, g_ref, w_ref, on_ref, oh_ref, ot_ref, xn_ref, *, n_nat, n_head):
    j = pl.program_id(1)

    @pl.when(j == 0)
    def _():
        xn_ref[...] = _rms(x_ref[...], g_ref[...]).astype(BF16)

    acc = jnp.dot(xn_ref[...], w_ref[...], preferred_element_type=F32)

    @pl.when(j < n_nat)
    def _():
        on_ref[...] = acc.astype(on_ref.dtype)

    @pl.when((j >= n_nat) & (j < n_nat + n_head))
    def _():
        dh = oh_ref.shape[3]
        for t in range(oh_ref.shape[0]):
            for gi in range(oh_ref.shape[1]):
                c0 = (t * oh_ref.shape[1] + gi) * dh
                oh_ref[t, gi] = acc[:, c0:c0 + dh].astype(oh_ref.dtype)

    @pl.when(j >= n_nat + n_head)
    def _():
        acc_t = acc.T
        for k in range(ot_ref.shape[0]):
            ot_ref[k] = acc_t[:, k * T_CHUNK:(k + 1) * T_CHUNK].astype(ot_ref.dtype)


def in_projection(x, g, w, n_nat, n_head, groups, dh, *, tm=2048, tn=512):
    m, k = x.shape
    n_t = w.shape[1] // tn - n_nat - n_head
    per_tile = tn // (groups * dh)
    assert w.shape[1] % tn == 0 and m % tm == 0 and tm % T_CHUNK == 0 and tn % (groups * dh) == 0
    return pl.pallas_call(
        functools.partial(_in_proj_kernel, n_nat=n_nat, n_head=n_head),
        grid=(m // tm, n_nat + n_head + n_t),
        in_specs=[
            pl.BlockSpec((tm, k), lambda i, j: (i, 0), pipeline_mode=pl.Buffered(1)),
            pl.BlockSpec((1, k), lambda i, j: (0, 0)),
            pl.BlockSpec((k, tn), lambda i, j: (0, j)),
        ],
        out_specs=[
            pl.BlockSpec((tm, tn), lambda i, j: (i, jnp.minimum(j, n_nat - 1))),
            pl.BlockSpec((per_tile, groups, tm, dh), lambda i, j: (jnp.clip(j - n_nat, 0, n_head - 1), 0, i, 0)),
            pl.BlockSpec((tm // T_CHUNK, tn, T_CHUNK),
                         lambda i, j: (i, jnp.maximum(j - n_nat - n_head, 0), 0)),
        ],
        out_shape=[jax.ShapeDtypeStruct((m, n_nat * tn), BF16),
                   jax.ShapeDtypeStruct((n_head * per_tile, groups, m, dh), BF16),
                   jax.ShapeDtypeStruct((m // T_CHUNK, n_t * tn, T_CHUNK), BF16)],
        scratch_shapes=[pltpu.VMEM((tm, k), BF16)],
        compiler_params=_params("parallel", "arbitrary"),
        name="in_projection",
    )(x, g.reshape(1, k), w)


def _mlp_kernel(x_ref, g_ref, w1_ref, w2_ref, gf_ref, o_ref, xn_ref, h_ref, *, final_norm):
    f = pl.program_id(1)

    @pl.when(f == 0)
    def _():
        x = x_ref[...]
        xn_ref[...] = _rms(x, g_ref[...]).astype(BF16)
        o_ref[...] = x
        h_ref[...] = jnp.zeros(h_ref.shape, BF16)

    h_new = jnp.dot(xn_ref[...], w1_ref[...].astype(BF16), preferred_element_type=F32)
    o_ref[...] += jnp.dot(h_ref[...], w2_ref[...].astype(BF16), preferred_element_type=F32)
    h_ref[...] = jnp.square(jnp.maximum(h_new, 0.0)).astype(BF16)

    if final_norm:
        @pl.when(f == pl.num_programs(1) - 1)
        def _():
            o_ref[...] = _rms(o_ref[...], gf_ref[...])


def fused_mlp(x, g, w1, w2, layer, gf=None, *, tm=2048, tf=256):
    m, d = x.shape
    nf = w1.shape[2] // tf
    final_norm = gf is not None
    if gf is None:
        gf = g
    once = pl.Buffered(1)
    return pl.pallas_call(
        functools.partial(_mlp_kernel, final_norm=final_norm),
        grid=(m // tm, nf + 1),
        in_specs=[
            pl.BlockSpec((tm, d), lambda i, f: (i, 0), pipeline_mode=once),
            pl.BlockSpec((1, d), lambda i, f: (0, 0)),
            pl.BlockSpec((None, d, tf), lambda i, f: (layer, 0, jnp.minimum(f, nf - 1))),
            pl.BlockSpec((None, tf, d), lambda i, f: (layer, jnp.maximum(f - 1, 0), 0)),
            pl.BlockSpec((1, d), lambda i, f: (0, 0)),
        ],
        out_specs=pl.BlockSpec((tm, d), lambda i, f: (i, 0), pipeline_mode=once),
        out_shape=jax.ShapeDtypeStruct((m, d), F32),
        scratch_shapes=[pltpu.VMEM((tm, d), BF16), pltpu.VMEM((tm, tf), BF16)],
        compiler_params=_params("parallel", "arbitrary"),
        name="fused_mlp",
    )(x, g.reshape(1, d), w1, w2, gf.reshape(1, d))


def _conv_kernel(a_ref, gt_ref, w_ref, b_ref, lg_ref, lb_ref, o_ref, cs_ref, y_ref, wb_ref, *, tt):
    i = pl.program_id(0)

    @pl.when(i == 0)
    def _():
        cs_ref[0, 0:CONV_HALO, :] = jnp.zeros((CONV_HALO, cs_ref.shape[2]), F32)
        for k in range(CONV_WIDTH):
            wb_ref[k] = jnp.broadcast_to(w_ref[k:k + 1, :], wb_ref.shape[1:])

    @pl.when(i > 0)
    def _():
        cs_ref[0, 0:CONV_HALO, :] = cs_ref[0, tt:tt + CONV_HALO, :]

    cs_ref[0, CONV_HALO:, :] = a_ref[...].astype(F32) * jax.nn.sigmoid(gt_ref[...].astype(F32))
    c_ext = cs_ref[0]
    for b in range(1, SUBLANES):
        cs_ref[b] = pltpu.roll(c_ext, b, axis=0)

    rows = 4 * SUBLANES

    def chunk(j, carry):
        r0 = pl.multiple_of(j * rows, rows)
        groups = range(rows // SUBLANES)
        acc = [jnp.zeros((SUBLANES, cs_ref.shape[2]), F32) + b_ref[...] for _ in groups]
        for k in range(CONV_WIDTH):
            a, b = divmod(k, SUBLANES)
            wk = wb_ref[CONV_WIDTH - 1 - k]
            for q in groups:
                lo = pl.multiple_of(r0 + CONV_HALO + (q - a) * SUBLANES, SUBLANES)
                acc[q] = acc[q] + cs_ref[b, pl.ds(lo, SUBLANES), :] * wk
        for q in groups:
            y_ref[pl.ds(pl.multiple_of(r0 + q * SUBLANES, SUBLANES), SUBLANES), :] = acc[q]
        return carry

    lax.fori_loop(0, tt // rows, chunk, 0)

    acc = y_ref[...]
    mu = jnp.mean(acc, axis=-1, keepdims=True)
    xc = acc - mu
    var = jnp.mean(xc * xc, axis=-1, keepdims=True)
    y = xc * lax.rsqrt(var + EPS) * lg_ref[...] + lb_ref[...]
    o_ref[...] = (y * jax.nn.sigmoid(y)).astype(o_ref.dtype)


def conv_module(proj, conv_w, conv_b, ln_g, ln_b, *, tt=512):
    s = proj.shape[0]
    c = conv_w.shape[1]
    row = lambda v: v.reshape(1, c)
    return pl.pallas_call(
        functools.partial(_conv_kernel, tt=tt),
        grid=(s // tt,),
        in_specs=[
            pl.BlockSpec((tt, c), lambda i: (i, 0)),
            pl.BlockSpec((tt, c), lambda i: (i, 1)),
            pl.BlockSpec((CONV_WIDTH, c), lambda i: (0, 0)),
            pl.BlockSpec((1, c), lambda i: (0, 0)),
            pl.BlockSpec((1, c), lambda i: (0, 0)),
            pl.BlockSpec((1, c), lambda i: (0, 0)),
        ],
        out_specs=pl.BlockSpec((tt, c), lambda i: (i, 0)),
        out_shape=jax.ShapeDtypeStruct((s, c), BF16),
        scratch_shapes=[pltpu.VMEM((SUBLANES, tt + CONV_HALO, c), F32), pltpu.VMEM((tt, c), F32),
                        pltpu.VMEM((CONV_WIDTH, SUBLANES, c), F32)],
        compiler_params=_params("arbitrary"),
        name="conv_module",
    )(proj, proj, conv_w, row(conv_b), row(ln_g), row(ln_b))


def _compress_kernel(kt_ref, pos_ref, w1_ref, w2_ref, o_ref):
    kt = kt_ref[0, 0].astype(F32)
    half = kt.shape[1]
    first = (kt + pos_ref[0, 0:1, :]).astype(BF16)
    second = (kt + pos_ref[0, 1:2, :]).astype(BF16)
    p = jnp.dot(first, w1_ref[0, 0:half, :], preferred_element_type=F32)
    q = jnp.dot(second, w1_ref[0, half:, :], preferred_element_type=F32)
    n = q.shape[0]
    h = p + pltpu.roll(q, n - 1, axis=0)
    h = h * jax.nn.sigmoid(h)
    o_ref[0, 0] = jnp.dot(h.astype(BF16), w2_ref[0], preferred_element_type=F32).astype(o_ref.dtype)


def compress(kt, pos, w1, w2):
    two, g, nch, half = kt.shape
    hid = w1.shape[2]
    dh = w2.shape[2]
    return pl.pallas_call(
        _compress_kernel,
        grid=(two, g),
        in_specs=[
            pl.BlockSpec((1, 1, nch, half), lambda a, b: (a, b, 0, 0)),
            pl.BlockSpec((1, 2, half), lambda a, b: (a, 0, 0)),
            pl.BlockSpec((1, 2 * half, hid), lambda a, b: (a, 0, 0)),
            pl.BlockSpec((1, hid, dh), lambda a, b: (a, 0, 0)),
        ],
        out_specs=pl.BlockSpec((1, 1, nch, dh), lambda a, b: (a, b, 0, 0)),
        out_shape=jax.ShapeDtypeStruct((two, g, nch, dh), BF16),
        compiler_params=_params("parallel", "parallel"),
        name="compress",
    )(kt, pos, w1, w2)


def _split3(x):
    hi = x.astype(BF16)
    r1 = x - hi.astype(F32)
    mid = r1.astype(BF16)
    lo = (r1 - mid.astype(F32)).astype(BF16)
    return hi, mid, lo


def _nsa_kernel(qt_ref, kc_ref, vct_ref, oh_ref, ks_ref, vst_ref, kw_ref, vwt_ref, glt_ref, o_ref,
                qa_ref, m_ref, acc_ref, s_ref, part_ref, *, seq):
    i = pl.program_id(1)
    s0 = i * Q_BLOCK
    dh = NSA_HEAD_DIM
    cols = NSA_GROUP * Q_BLOCK
    pair = 2 * Q_BLOCK
    n_sel = seq // SEL_LEN
    n_cmp = kc_ref.shape[1]

    def with_ones(vt):
        return jnp.concatenate([vt, jnp.ones((ONES_ROWS, vt.shape[1]), BF16)], axis=0)

    qt = jnp.concatenate([qt_ref[0, r * dh:(r + 1) * dh, :] for r in range(NSA_GROUP)], axis=1)
    qt = (qt.astype(F32) * (dh ** -0.5 * LOG2E)).astype(BF16)
    t_col = s0 + (lax.broadcasted_iota(jnp.int32, (1, cols), 1) & (Q_BLOCK - 1))

    sc = jnp.dot(kc_ref[0], qt, preferred_element_type=F32)
    cmp_end = lax.broadcasted_iota(jnp.int32, (n_cmp, 1), 0) * CMP_STRIDE + (CMP_LEN - 1)
    sc = jnp.where(cmp_end <= t_col, sc, NEG_INF)
    e = jnp.exp2(sc - jnp.max(sc, axis=0, keepdims=True))
    l = jnp.sum(e, axis=0, keepdims=True)
    pc = e * jnp.where(t_col >= CMP_LEN - 1, 1.0 / l, 0.0)
    oc = jnp.dot(vct_ref[0], pc.astype(BF16), preferred_element_type=F32)

    imp_c = pc[:, 0:Q_BLOCK]
    for r in range(1, NSA_GROUP):
        imp_c = imp_c + pc[:, r * Q_BLOCK:(r + 1) * Q_BLOCK]
    sj = lax.broadcasted_iota(jnp.int32, (n_sel, n_cmp), 0) * SEL_LEN
    cn = lax.broadcasted_iota(jnp.int32, (n_sel, n_cmp), 1) * CMP_STRIDE
    overlap = jnp.where((cn <= sj + SEL_LEN - 1) & (cn + CMP_LEN - 1 >= sj), 1.0, 0.0).astype(BF16)
    imp = jnp.zeros((n_sel, Q_BLOCK), F32)
    for part in _split3(imp_c):
        imp = imp + jnp.dot(overlap, part, preferred_element_type=F32)

    t_win = t_col + (jnp.min(imp, keepdims=True) < 0.0).astype(jnp.int32)
    wlen = WINDOW + Q_BLOCK
    w0 = pl.multiple_of(jnp.maximum(s0 - WINDOW, 0), Q_BLOCK)
    wc = w0 >> (T_CHUNK.bit_length() - 1)
    sw = jnp.dot(kw_ref[0, pl.ds(w0, wlen), :], qt, preferred_element_type=F32)
    wpos = w0 + lax.broadcasted_iota(jnp.int32, (wlen, 1), 0)
    sw = jnp.where(wpos <= t_win, sw, NEG_INF)
    sw = jnp.concatenate([jnp.where(wpos[0:Q_BLOCK] > t_win - WINDOW, sw[0:Q_BLOCK], NEG_INF), sw[Q_BLOCK:]],
                         axis=0)
    pw = jnp.exp2((sw - jnp.max(sw, axis=0, keepdims=True)).astype(BF16))
    accw = jnp.zeros((dh + ONES_ROWS, cols), F32)
    for j in range(wlen // T_CHUNK):
        accw = accw + jnp.dot(with_ones(vwt_ref[wc + j]), pw[j * T_CHUNK:(j + 1) * T_CHUNK],
                              preferred_element_type=F32)
    ow = accw[0:dh] * (1.0 / accw[dh:dh + 1])

    def gate_row(b):
        gate = jax.nn.sigmoid(glt_ref[0].astype(F32))
        return jnp.concatenate(
            [gate[N_BRANCH * r + b:N_BRANCH * r + b + 1] for r in range(NSA_GROUP)], axis=1)

    part_ref[...] = gate_row(0) * oc + gate_row(2) * ow


    jj = lax.broadcasted_iota(jnp.int32, (n_sel, Q_BLOCK), 0)
    cur = (s0 + lax.broadcasted_iota(jnp.int32, (1, Q_BLOCK), 1)) >> (SEL_LEN.bit_length() - 1)
    forced = (jj == 0) | (jj == cur) | (jj == cur - 1)
    future = jj > cur
    assert FORCE_BONUS > 2 * NSA_GROUP
    n_forced = 3
    work = jnp.where(forced, -jnp.inf, jnp.where(future, NEG_INF, imp))
    jf = jj.astype(F32)
    chosen = jnp.where(forced, 1.0, 0.0)
    for _ in range(min(SEL_TOPK, n_sel) - n_forced):
        mx = jnp.max(work, axis=0, keepdims=True)
        first = jnp.min(jnp.where(work == mx, jf, float(n_sel)), axis=0, keepdims=True)
        hit = jf == first
        chosen = jnp.where(hit, 1.0, chosen)
        work = jnp.where(hit, -jnp.inf, work)
    bias = jnp.where((chosen > 0.0) & jnp.logical_not(future), 0.0, NEG_INF).astype(BF16)
    qa_ref[0:n_sel, :] = jnp.concatenate([bias] * NSA_GROUP, axis=1)
    qa_ref[n_sel:, :] = qt

    m_ref[...] = jnp.full(m_ref.shape, NEG_INF, F32)
    acc_ref[...] = jnp.zeros(acc_ref.shape, F32)

    def scores(c, dst_ref):
        k0 = pl.multiple_of(c * SEL_CHUNK, SEL_CHUNK)
        ka = jnp.concatenate([oh_ref[pl.ds(k0, SEL_CHUNK), :], ks_ref[0, pl.ds(k0, SEL_CHUNK), :]],
                             axis=1)
        for h0 in range(0, cols, pair):
            dst_ref[:, h0:h0 + pair] = jnp.dot(ka, qa_ref[:, h0:h0 + pair], preferred_element_type=F32)

    def accumulate(c, src_ref, causal, n_keys=SEL_CHUNK):
        s = src_ref[0:n_keys, :]
        if causal:
            kpos = c * SEL_CHUNK + lax.broadcasted_iota(jnp.int32, (n_keys, 1), 0)
            s = jnp.where(kpos <= t_col, s, NEG_INF)
        m_old = m_ref[...]
        m_new = jnp.maximum(m_old, jnp.max(s, axis=0, keepdims=True))
        p = jnp.exp2(s - m_new).astype(BF16)
        c0 = c * (SEL_CHUNK // T_CHUNK)
        vt = jnp.concatenate([vst_ref[c0 + k] for k in range(n_keys // T_CHUNK)], axis=1)
        pv = jnp.dot(with_ones(vt), p, preferred_element_type=F32)
        acc_ref[...] = acc_ref[...] * jnp.exp2(m_old - m_new) + pv
        m_ref[...] = m_new

    n_full = s0 >> (SEL_CHUNK.bit_length() - 1)
    sa_ref, sb_ref = s_ref.at[0], s_ref.at[1]
    scores(0, sa_ref)

    def body(j, carry):
        c = 2 * j
        scores(c + 1, sb_ref)
        accumulate(c, sa_ref, False)
        scores(c + 2, sa_ref)
        accumulate(c + 1, sb_ref, False)
        return carry

    lax.fori_loop(0, n_full >> 1, body, 0)

    @pl.when((n_full & 1) == 1)
    def _():
        scores(n_full, sb_ref)
        accumulate(n_full - 1, sa_ref, False)

    own = s0 - n_full * SEL_CHUNK
    for k in range(SEL_CHUNK // Q_BLOCK):
        @pl.when(own == k * Q_BLOCK)
        def _(k=k):
            accumulate(n_full, s_ref.at[n_full & 1], True, (k + 1) * Q_BLOCK)

    acc = acc_ref[...]
    os_ = acc[0:dh] * (1.0 / acc[dh:dh + 1])

    out_t = part_ref[...] + gate_row(1) * os_
    out_t = jnp.concatenate([out_t[:, r * Q_BLOCK:(r + 1) * Q_BLOCK] for r in range(NSA_GROUP)], axis=0)
    o_ref[...] = out_t.T.astype(o_ref.dtype)


def nsa_attention(proj_t, rows, kc, vct, onehot, proj_h, ks_id, kw_id):
    nt = proj_t.shape[0]
    s = nt * T_CHUNK
    g, _, dh = kc.shape
    q_row, vs_row, vw_row, gate_row = rows
    assert Q_BLOCK == T_CHUNK and all(r % (NSA_GROUP * dh) == 0 for r in (q_row, vs_row, vw_row))
    assert gate_row % GATE_SLOT == 0
    dv = dh + ONES_ROWS
    cols = NSA_GROUP * Q_BLOCK
    whole = lambda a: pl.BlockSpec((1,) + a.shape[1:], lambda gi, i: (gi,) + (0,) * (a.ndim - 1))
    v_t = lambda row: pl.BlockSpec((nt, dh, T_CHUNK), lambda gi, i: (0, row // dh + gi, 0))
    k_h = lambda t: pl.BlockSpec((None, 1, s, dh), lambda gi, i: (t, gi, 0, 0))
    return pl.pallas_call(
        functools.partial(_nsa_kernel, seq=s),
        grid=(g, s // Q_BLOCK),
        in_specs=[
            pl.BlockSpec((1, NSA_GROUP * dh, Q_BLOCK), lambda gi, i: (i, q_row // (NSA_GROUP * dh) + gi, 0)),
            whole(kc), whole(vct), pl.BlockSpec(onehot.shape, lambda gi, i: (0, 0)),
            k_h(ks_id), v_t(vs_row), k_h(kw_id), v_t(vw_row),
            pl.BlockSpec((1, GATE_SLOT, Q_BLOCK), lambda gi, i: (i, gate_row // GATE_SLOT + gi, 0)),
        ],
        out_specs=pl.BlockSpec((Q_BLOCK, NSA_GROUP * dh), lambda gi, i: (i, gi)),
        out_shape=jax.ShapeDtypeStruct((s, g * NSA_GROUP * dh), BF16),
        scratch_shapes=[pltpu.VMEM((onehot.shape[1] + dh, cols), BF16),
                        pltpu.VMEM((1, cols), F32),
                        pltpu.VMEM((dv, cols), F32),
                        pltpu.VMEM((2, SEL_CHUNK, cols), F32),
                        pltpu.VMEM((dh, cols), F32)],
        compiler_params=_params("parallel", "arbitrary"),
        name="nsa_attention",
    )(proj_t, kc, vct, onehot, proj_h, proj_t, proj_h, proj_t, proj_t)


def _out_proj_kernel(c_ref, o_ref, w_ref, x_ref, y_ref):
    half = c_ref.shape[1]
    acc = jnp.dot(c_ref[...], w_ref[0:half, :], preferred_element_type=F32)
    acc = acc + jnp.dot(o_ref[...], w_ref[half:, :], preferred_element_type=F32)
    y_ref[...] = x_ref[...] + acc


def out_proj(c, o, w, x, *, tm=1024, tn=512):
    m, half = c.shape
    n = w.shape[1]
    return pl.pallas_call(
        _out_proj_kernel,
        grid=(m // tm, n // tn),
        in_specs=[
            pl.BlockSpec((tm, half), lambda i, j: (i, 0)),
            pl.BlockSpec((tm, half), lambda i, j: (i, 0)),
            pl.BlockSpec((2 * half, tn), lambda i, j: (0, j)),
            pl.BlockSpec((tm, tn), lambda i, j: (i, j)),
        ],
        out_specs=pl.BlockSpec((tm, tn), lambda i, j: (i, j)),
        out_shape=jax.ShapeDtypeStruct((m, n), F32),
        compiler_params=_params("parallel", "parallel"),
        name="out_proj",
    )(c, o, w, x)


def _pool_kernel(x_ref, halo_ref, g_ref, w_ref, sc_ref, o_ref, *, tm):
    i = pl.program_id(0)
    x = x_ref[...]
    xn = _rms(x, g_ref[...])
    hn = _rms(halo_ref[...], g_ref[...]) * jnp.where(i > 0, 1.0, 0.0)
    ext = jnp.concatenate([hn, xn], axis=0)
    pd = w_ref.shape[1]
    pos1 = i * tm + 1 + lax.broadcasted_iota(jnp.int32, (tm, 1), 0)
    for gi, win in enumerate(POOL_WINDOWS):
        sl = slice(gi * pd, (gi + 1) * pd)
        s = ext[:, sl]
        sh = 1
        while sh < win:
            s = s + pltpu.roll(s, sh, axis=0)
            sh *= 2
        cnt = jnp.minimum(pos1, win).astype(F32)
        p = s[POOL_HALO:] / cnt - xn[:, sl]
        y = jnp.dot(p.astype(BF16), w_ref[gi], preferred_element_type=F32)
        o_ref[:, sl] = x[:, sl] + y * sc_ref[:, sl]


def pool_mixer(x, g, pool_w, pool_scale, *, tm=512):
    m, d = x.shape
    ng, pd, _ = pool_w.shape
    hb = tm // POOL_HALO
    return pl.pallas_call(
        functools.partial(_pool_kernel, tm=tm),
        grid=(m // tm,),
        in_specs=[
            pl.BlockSpec((tm, d), lambda i: (i, 0)),
            pl.BlockSpec((POOL_HALO, d), lambda i: (jnp.maximum(i * hb - 1, 0), 0)),
            pl.BlockSpec((1, d), lambda i: (0, 0)),
            pl.BlockSpec((ng, pd, pd), lambda i: (0, 0, 0)),
            pl.BlockSpec((1, d), lambda i: (0, 0)),
        ],
        out_specs=pl.BlockSpec((tm, d), lambda i: (i, 0)),
        out_shape=jax.ShapeDtypeStruct((m, d), F32),
        compiler_params=_params("parallel"),
        name="pool_mixer",
    )(x, x, g.reshape(1, d), pool_w, pool_scale.reshape(1, d))


def _even_mixer(x, g, w_in, w_out, conv_w, conv_b, ln_g, ln_b, pos_k, pos_v, kw1, kw2, vw1, vw2):
    s, d = x.shape
    dh, gk, hq = NSA_HEAD_DIM, NSA_KV_HEADS, NSA_HEADS
    conv_cols = 2 * conv_w.shape[1]
    q_cols, kv_cols = hq * dh, gk * dh
    tn = 512
    q0 = conv_cols
    kv = lambda k: w_in[:, q0 + q_cols + k * kv_cols:q0 + q_cols + (k + 1) * kv_cols]
    w_gate = w_in[:, q0 + q_cols + 6 * kv_cols:].reshape(d, gk, NSA_GROUP * N_BRANCH)
    w_gate = jnp.pad(w_gate, ((0, 0), (0, 0), (0, GATE_SLOT - NSA_GROUP * N_BRANCH))).reshape(d, gk * GATE_SLOT)
    head = [kv(0), kv(1), kv(2), kv(4)]
    feat = [w_in[:, q0:q0 + q_cols], kv(3), kv(5), w_gate]
    n_feat_cols = sum(w.shape[1] for w in feat)
    assert conv_cols % tn == 0 and (len(head) * kv_cols) % tn == 0
    w_all = jnp.concatenate([w_in[:, :q0]] + head + feat + [jnp.zeros((d, (-n_feat_cols) % tn), F32)],
                            axis=1).astype(BF16)

    proj, proj_h, proj_t = in_projection(x, g, w_all, conv_cols // tn, len(head) * kv_cols // tn, gk, dh, tn=tn)
    c = conv_module(proj, conv_w, conv_b, ln_g, ln_b)

    nch = s // CMP_STRIDE
    kt = proj_h[0:2].reshape(2, gk, nch, CMP_STRIDE * dh)
    pos = jnp.stack([pos_k, pos_v]).reshape(2, 2, CMP_STRIDE * dh)
    kvc = compress(kt, pos, jnp.stack([kw1, vw1]).astype(BF16), jnp.stack([kw2, vw2]).astype(BF16))

    n_sel = s // SEL_LEN
    onehot = (jnp.arange(s)[:, None] // SEL_LEN == jnp.arange(n_sel)[None, :]).astype(BF16)
    rows = (0, q_cols, q_cols + kv_cols, q_cols + 2 * kv_cols)
    o = nsa_attention(proj_t, rows, kvc[0], kvc[1].transpose(0, 2, 1), onehot, proj_h, 2, 3)
    return out_proj(c, o, w_out.astype(BF16), x)


def kernel(x, mix_norm, mlp_norm, w_mlp_in, w_mlp_out, w_in, w_out, conv_w, conv_b, conv_ln_g,
           conv_ln_b, cmp_pos_k, cmp_pos_v, cmp_k_w1, cmp_k_w2, cmp_v_w1, cmp_v_w2, pool_w,
           pool_scale, final_norm):
    b, s, d = x.shape
    assert b == 1
    depth = mix_norm.shape[0]
    h = x.reshape(s, d)
    for layer in range(depth):
        i = layer // 2
        if layer % 2 == 0:
            h = _even_mixer(h, mix_norm[layer], w_in[i], w_out[i], conv_w[i], conv_b[i],
                            conv_ln_g[i], conv_ln_b[i], cmp_pos_k[i], cmp_pos_v[i],
                            cmp_k_w1[i], cmp_k_w2[i], cmp_v_w1[i], cmp_v_w2[i])
        else:
            h = pool_mixer(h, mix_norm[layer], pool_w[i].astype(BF16), pool_scale[i])
        fg = final_norm if layer == depth - 1 else None
        h = fused_mlp(h, mlp_norm[layer], w_mlp_in, w_mlp_out, layer, fg)
    return h.reshape(b, s, d)
```
